```python
import jax, jax.numpy as jnp
from jax import lax
import numpy as np

D_MODEL = 1024
BATCH = 8
SEQ = 2048
DEPTH = 4
DEC_BATCH = 128
DEC_SEQ = 4
PAST_LEN = 2048
PAGE_SIZE = 128

N_EVEN = (DEPTH + 1) // 2
N_ODD = DEPTH // 2
HEAD_DIM = 64
N_ATTN_HEADS = (D_MODEL // 2) // HEAD_DIM
ATTN_WIDTH = N_ATTN_HEADS * HEAD_DIM
POOL_WIDTH = D_MODEL // 2
POOL_WINDOWS = (2, 4, 8, 16)
N_POOL_GROUPS = len(POOL_WINDOWS)
POOL_GROUP = POOL_WIDTH // N_POOL_GROUPS
POOL_HIST = max(POOL_WINDOWS) - 1
MOBA_BLOCK = 256
MOBA_TOPK = 3
PROMPT_Q_BLOCK = 16
SAMPLE_Q_BLOCK = 1
ROPE_THETA = 10000.0
SGU_WIDTH = D_MODEL
SGU_CHUNK = 128
N_SGU_GROUPS = 8
SGU_GROUP = SGU_WIDTH // N_SGU_GROUPS
D_FF = 4 * D_MODEL
NORM_EPS = 1e-6
LN_EPS = 1e-5

kernel_name = 'moba_pool_gmlp_hybrid_step'


def rms_norm(x, g):
    xf = x.astype(jnp.float32)
    y = xf * lax.rsqrt(jnp.mean(xf * xf, axis=-1, keepdims=True) + NORM_EPS)
    return (y * g.astype(jnp.float32)).astype(x.dtype)


def layer_norm(x, g, b):
    xf = x.astype(jnp.float32)
    mu = jnp.mean(xf, axis=-1, keepdims=True)
    var = jnp.mean(jnp.square(xf - mu), axis=-1, keepdims=True)
    y = (xf - mu) * lax.rsqrt(var + LN_EPS)
    return (y * g.astype(jnp.float32) + b.astype(jnp.float32)).astype(x.dtype)


def rotary(x, pos0):
    s, dh = x.shape[1], x.shape[-1]
    pos = jnp.arange(s, dtype=jnp.float32) + pos0
    inv = ROPE_THETA ** (-jnp.arange(0, dh, 2, dtype=jnp.float32) / dh)
    ang = pos[:, None] * inv[None, :]
    cos = jnp.cos(ang)[None, :, None, :]
    sin = jnp.sin(ang)[None, :, None, :]
    xf = x.astype(jnp.float32)
    x1, x2 = xf[..., : dh // 2], xf[..., dh // 2:]
    return jnp.concatenate([x1 * cos - x2 * sin, x2 * cos + x1 * sin], axis=-1).astype(x.dtype)


def moba_attention(q, k, v, pos0, q_block):
    b, sq, h, dh = q.shape
    t = k.shape[1]
    nb = -(-t // MOBA_BLOCK)
    pad = nb * MOBA_BLOCK - t
    k = jnp.pad(k, ((0, 0), (0, pad), (0, 0), (0, 0)))
    v = jnp.pad(v, ((0, 0), (0, pad), (0, 0), (0, 0)))
    kb = k.reshape(b, nb, MOBA_BLOCK, h, dh).transpose(0, 3, 1, 2, 4)
    vb = v.reshape(b, nb, MOBA_BLOCK, h, dh).transpose(0, 3, 1, 2, 4)
    k_mean = jnp.mean(kb.astype(jnp.float32), axis=3)
    n_top = min(MOBA_TOPK, nb)
    blk = jnp.arange(nb, dtype=jnp.int32)
    offs = jnp.arange(MOBA_BLOCK, dtype=jnp.int32)
    gather = jax.vmap(jax.vmap(lambda blocks, ix: blocks[ix]))
    scale = dh ** -0.5

    def attend_block(c):
        qc = lax.dynamic_slice_in_dim(q, c * q_block, q_block, axis=1)
        qpos = pos0 + c * q_block + jnp.arange(q_block, dtype=jnp.int32)
        own = qpos // MOBA_BLOCK
        gate = jnp.einsum('bqhd,bhnd->bhqn', qc.astype(jnp.float32), k_mean)
        gate = jnp.where(blk[None, None, None, :] < own[None, None, :, None], gate, -jnp.inf)
        _, top = lax.top_k(gate, n_top)
        own_idx = jnp.broadcast_to(own[None, None, :, None], (b, h, q_block, 1))
        idx = jnp.concatenate([top.astype(jnp.int32), own_idx], axis=-1)
        slot_ok = jnp.concatenate([jnp.arange(n_top)[None, :] < own[:, None],
                                   jnp.ones((q_block, 1), dtype=bool)], axis=-1)
        kg = gather(kb, idx)
        vg = gather(vb, idx)
        key_pos = idx[..., None] * MOBA_BLOCK + offs
        mask = slot_ok[None, None, :, :, None] & (key_pos <= qpos[None, None, :, None, None])
        s = jnp.einsum('bqhd,bhqrkd->bhqrk', qc, kg, preferred_element_type=jnp.float32) * scale
        s = jnp.where(mask, s, -jnp.inf)
        p = jax.nn.softmax(s.reshape(b, h, q_block, -1), axis=-1).reshape(s.shape)
        return jnp.einsum('bhqrk,bhqrkd->bqhd', p.astype(vg.dtype), vg)

    out = lax.map(attend_block, jnp.arange(sq // q_block, dtype=jnp.int32))
    return out.transpose(1, 0, 2, 3, 4).reshape(b, sq, h, dh)


def multiscale_pool(p, hist, pos0):
    b, s, c = p.shape
    full = jnp.concatenate([hist.astype(p.dtype), p], axis=1)
    ff = full.astype(jnp.float32)
    cs = jnp.concatenate([jnp.zeros((b, 1, c), jnp.float32), jnp.cumsum(ff, axis=1)], axis=1)
    end = cs[:, POOL_HIST + 1:]
    pos = jnp.arange(s, dtype=jnp.int32) + pos0
    outs = []
    for g, w in enumerate(POOL_WINDOWS):
        sl = slice(g * POOL_GROUP, (g + 1) * POOL_GROUP)
        start = cs[:, POOL_HIST + 1 - w: POOL_HIST + 1 - w + s, sl]
        count = jnp.minimum(pos + 1, w).astype(jnp.float32)[None, :, None]
        outs.append((end[..., sl] - start) / count - ff[:, POOL_HIST:, sl])
    pooled = jnp.concatenate(outs, axis=-1).astype(p.dtype)
    return pooled, full[:, -POOL_HIST:]


def moba_pool_mixer(xn, w_in, w_o, w_pool, pool_scale, pos0, k_past, v_past, pool_hist, q_block):
    b, s, _ = xn.shape
    proj = xn @ w_in
    q, k, v, p = jnp.split(proj, [ATTN_WIDTH, 2 * ATTN_WIDTH, 3 * ATTN_WIDTH], axis=-1)
    q = rotary(q.reshape(b, s, N_ATTN_HEADS, HEAD_DIM), pos0)
    k = rotary(k.reshape(b, s, N_ATTN_HEADS, HEAD_DIM), pos0)
    v = v.reshape(b, s, N_ATTN_HEADS, HEAD_DIM)
    if k_past is None:
        k_all, v_all = k, v
    else:
        k_all = jnp.concatenate([k_past.astype(k.dtype), k], axis=1)
        v_all = jnp.concatenate([v_past.astype(v.dtype), v], axis=1)
    attn = moba_attention(q, k_all, v_all, pos0, q_block).reshape(b, s, ATTN_WIDTH)
    pooled, new_hist = multiscale_pool(p, pool_hist, pos0)
    pooled = jnp.einsum('bsgc,gcd->bsgd', pooled.reshape(b, s, N_POOL_GROUPS, POOL_GROUP), w_pool)
    pooled = pooled.reshape(b, s, POOL_WIDTH) * pool_scale
    out = jnp.concatenate([attn, pooled], axis=-1) @ w_o
    return out, k, v, new_hist


def chunk_gmlp(xn, w_in, ln_g, ln_b, w_s, b_s, w_o):
    b, s, _ = xn.shape
    z = jax.nn.gelu(xn @ w_in)
    u, v = jnp.split(z, 2, axis=-1)
    v = layer_norm(v, ln_g, ln_b)
    L = SGU_CHUNK if s >= SGU_CHUNK else s
    w = jnp.where(jnp.tril(jnp.ones((L, L), dtype=bool)), w_s[:, :L, :L], 0.0).astype(v.dtype)
    vc = v.reshape(b, s // L, L, N_SGU_GROUPS, SGU_GROUP)
    mixed = jnp.einsum('gts,bnsgc->bntgc', w, vc) + b_s[:, :L].T[None, None, :, :, None]
    out = u * mixed.reshape(b, s, SGU_WIDTH)
    return out @ w_o, v


def sq_relu_mlp(xn, w_up, w_down):
    h = jax.nn.relu(xn @ w_up)
    return (h * h) @ w_down


def setup_inputs(seed: int = 0) -> dict:
    key = jax.random.key(seed)
    ks = jax.random.split(key, 21)
    f32 = jnp.float32
    n_pages = PAST_LEN // PAGE_SIZE
    n_phys = (5 * DEC_BATCH * n_pages) // 4

    def nrm(k, shape, scale=1.0):
        return jax.random.normal(k, shape, f32) * scale

    page_table = jax.random.permutation(ks[5], n_phys)[: DEC_BATCH * n_pages]
    page_table = page_table.reshape(DEC_BATCH, n_pages).astype(jnp.int32)
    return {
        'x_prompt': nrm(ks[0], (BATCH, SEQ, D_MODEL)),
        'x_sample': nrm(ks[1], (DEC_BATCH, DEC_SEQ, D_MODEL)),
        'cache_k': nrm(ks[2], (N_EVEN, n_phys, PAGE_SIZE, N_ATTN_HEADS, HEAD_DIM)),
        'cache_v': nrm(ks[3], (N_EVEN, n_phys, PAGE_SIZE, N_ATTN_HEADS, HEAD_DIM)),
        'state_pool': nrm(ks[4], (N_EVEN, DEC_BATCH, POOL_HIST, POOL_WIDTH)),
        'page_table': page_table,
        'norm_mix': 1.0 + 0.02 * nrm(ks[6], (DEPTH, D_MODEL)),
        'w_in_even': nrm(ks[7], (N_EVEN, D_MODEL, 3 * ATTN_WIDTH + POOL_WIDTH), D_MODEL ** -0.5),
        'w_o_even': nrm(ks[8], (N_EVEN, ATTN_WIDTH + POOL_WIDTH, D_MODEL), (ATTN_WIDTH + POOL_WIDTH) ** -0.5),
        'w_pool': nrm(ks[9], (N_EVEN, N_POOL_GROUPS, POOL_GROUP, POOL_GROUP), POOL_GROUP ** -0.5),
        'pool_scale': 1.0 + 0.02 * nrm(ks[10], (N_EVEN, POOL_WIDTH)),
        'w_in_odd': nrm(ks[11], (N_ODD, D_MODEL, 2 * SGU_WIDTH), D_MODEL ** -0.5),
        'sgu_ln_g': 1.0 + 0.02 * nrm(ks[12], (N_ODD, SGU_WIDTH)),
        'sgu_ln_b': 0.02 * nrm(ks[13], (N_ODD, SGU_WIDTH)),
        'sgu_w': nrm(ks[14], (N_ODD, N_SGU_GROUPS, SGU_CHUNK, SGU_CHUNK), SGU_CHUNK ** -0.5),
        'sgu_b': 1.0 + 0.02 * nrm(ks[15], (N_ODD, N_SGU_GROUPS, SGU_CHUNK)),
        'w_o_odd': nrm(ks[16], (N_ODD, SGU_WIDTH, D_MODEL), SGU_WIDTH ** -0.5),
        'norm_ffn': 1.0 + 0.02 * nrm(ks[17], (DEPTH, D_MODEL)),
        'w_ffn_up': nrm(ks[18], (DEPTH, D_MODEL, D_FF), D_MODEL ** -0.5),
        'w_ffn_down': nrm(ks[19], (DEPTH, D_FF, D_MODEL), D_FF ** -0.5),
        'norm_final': 1.0 + 0.02 * nrm(ks[20], (D_MODEL,)),
    }


def reference(x_prompt, x_sample, cache_k, cache_v, state_pool, page_table,
              norm_mix, w_in_even, w_o_even, w_pool, pool_scale,
              w_in_odd, sgu_ln_g, sgu_ln_b, sgu_w, sgu_b, w_o_odd,
              norm_ffn, w_ffn_up, w_ffn_down, norm_final):
    n_seq, n_pages = page_table.shape
    past_len = n_pages * cache_k.shape[2]
    yp, ys = x_prompt, x_sample
    prompt_k, prompt_v, sample_k, sample_v = [], [], [], []
    prompt_pool, sample_pool, sample_sgu = [], [], []
    for l in range(DEPTH):
        hp = rms_norm(yp, norm_mix[l])
        hs = rms_norm(ys, norm_mix[l])
        if l % 2 == 0:
            i = l // 2
            zero_hist = jnp.zeros((yp.shape[0], POOL_HIST, POOL_WIDTH), hp.dtype)
            op, kp_new, vp_new, pool_p = moba_pool_mixer(
                hp, w_in_even[i], w_o_even[i], w_pool[i], pool_scale[i],
                0, None, None, zero_hist, PROMPT_Q_BLOCK)
            k_past = cache_k[i][page_table].reshape(n_seq, past_len, N_ATTN_HEADS, HEAD_DIM)
            v_past = cache_v[i][page_table].reshape(n_seq, past_len, N_ATTN_HEADS, HEAD_DIM)
            os_, ks_new, vs_new, pool_s = moba_pool_mixer(
                hs, w_in_even[i], w_o_even[i], w_pool[i], pool_scale[i],
                past_len, k_past, v_past, state_pool[i], SAMPLE_Q_BLOCK)
            prompt_k.append(kp_new)
            prompt_v.append(vp_new)
            sample_k.append(ks_new)
            sample_v.append(vs_new)
            prompt_pool.append(pool_p)
            sample_pool.append(pool_s)
        else:
            j = l // 2
            op, _ = chunk_gmlp(hp, w_in_odd[j], sgu_ln_g[j], sgu_ln_b[j], sgu_w[j], sgu_b[j], w_o_odd[j])
            os_, v_rows = chunk_gmlp(hs, w_in_odd[j], sgu_ln_g[j], sgu_ln_b[j], sgu_w[j], sgu_b[j], w_o_odd[j])
            sample_sgu.append(v_rows)
        yp = yp + op
        ys = ys + os_
        yp = yp + sq_relu_mlp(rms_norm(yp, norm_ffn[l]), w_ffn_up[l], w_ffn_down[l])
        ys = ys + sq_relu_mlp(rms_norm(ys, norm_ffn[l]), w_ffn_up[l], w_ffn_down[l])
    y_prompt = rms_norm(yp, norm_final)
    y_sample = rms_norm(ys, norm_final)
    return (y_prompt, y_sample,
            jnp.stack(prompt_k), jnp.stack(prompt_v),
            jnp.stack(sample_k), jnp.stack(sample_v),
            jnp.stack(prompt_pool), jnp.stack(sample_pool),
            jnp.stack(sample_sgu))
```

```python
import functools

import jax
import jax.numpy as jnp
from jax import lax
from jax.experimental import pallas as pl
from jax.experimental.pallas import tpu as pltpu

F32 = jnp.float32
BF16 = jnp.bfloat16

D_MODEL = 1024
HEAD_DIM = 64
N_HEADS = 8
ATTN_WIDTH = 512
POOL_WIDTH = 512
POOL_WINDOWS = (2, 4, 8, 16)
POOL_GROUP = 128
POOL_HIST = 15
MOBA_BLOCK = 256
MOBA_TOPK = 3
ROPE_THETA = 10000.0
SGU_CHUNK = 128
N_SGU_GROUPS = 8
D_FF = 4096
NORM_EPS = 1e-6
LN_EPS = 1e-5

ROW_TILE = 512
FF_CHUNK = 1024
HALO = 16
MASK_BIAS = -1e30
VMEM_LIMIT = 56 * 1024 * 1024

_NT = (((1,), (1,)), ((), ()))


def _rms(x, g):
    ms = jnp.mean(x * x, axis=-1, keepdims=True)
    return x * lax.rsqrt(ms + NORM_EPS) * g


def _ffn(y, g_ref, wup_ref, wdn_ref):
    hn = _rms(y, g_ref[...]).astype(BF16)
    acc = y
    for c in range(D_FF // FF_CHUNK):
        sl = slice(c * FF_CHUNK, (c + 1) * FF_CHUNK)
        h = jnp.dot(hn, wup_ref[:, sl], preferred_element_type=F32)
        h = jnp.maximum(h, 0.0)
        h = (h * h).astype(BF16)
        acc = acc + jnp.dot(h, wdn_ref[sl, :], preferred_element_type=F32)
    return acc


def _const_spec(shape):
    nd = len(shape)
    return pl.BlockSpec(shape, lambda *_: (0,) * nd, pipeline_mode=pl.Buffered(1))


def _inproj_even_kernel(x_ref, g_ref, w_ref, cos_ref, sin_ref,
                        q_ref, k_ref, v_ref, p_ref):
    xn = _rms(x_ref[...], g_ref[...]).astype(BF16)
    proj = jnp.dot(xn, w_ref[...], preferred_element_type=F32)
    cos = jnp.concatenate([cos_ref[...]] * 4, axis=1)
    sin = jnp.concatenate([sin_ref[...]] * 4, axis=1)
    lane = lax.broadcasted_iota(jnp.int32, cos.shape, 1) % HEAD_DIM
    first_half = lane < HEAD_DIM // 2

    def rot(a):
        partner = jnp.where(first_half,
                            pltpu.roll(a, ATTN_WIDTH - HEAD_DIM // 2, 1),
                            pltpu.roll(a, HEAD_DIM // 2, 1))
        return a * cos + partner * sin

    q_ref[...] = rot(proj[:, 0:ATTN_WIDTH]) * (HEAD_DIM ** -0.5)
    k_ref[...] = rot(proj[:, ATTN_WIDTH:2 * ATTN_WIDTH])
    v_ref[...] = proj[:, 2 * ATTN_WIDTH:3 * ATTN_WIDTH]
    p_ref[...] = proj[:, 3 * ATTN_WIDTH:]


def _inproj_even(x, g, w, cos, sin, name):
    n = x.shape[0]
    tm = ROW_TILE
    n_pos_blocks = cos.shape[0] // tm
    row = lambda i: (i, 0)
    out = jax.ShapeDtypeStruct((n, ATTN_WIDTH), F32)
    return pl.pallas_call(
        _inproj_even_kernel,
        grid=(n // tm,),
        in_specs=[
            pl.BlockSpec((tm, D_MODEL), row),
            _const_spec((1, D_MODEL)),
            _const_spec(w.shape),
            pl.BlockSpec((tm, 128), lambda i: (i % n_pos_blocks, 0)),
            pl.BlockSpec((tm, 128), lambda i: (i % n_pos_blocks, 0)),
        ],
        out_specs=[pl.BlockSpec((tm, ATTN_WIDTH), row)] * 4,
        out_shape=[out] * 4,
        compiler_params=pltpu.CompilerParams(
            dimension_semantics=("arbitrary",), vmem_limit_bytes=VMEM_LIMIT),
        name=name,
    )(x, g, w, cos, sin)


def _top3_mask(gate, cand, lane_f):
    g = jnp.where(cand, gate, -jnp.inf)
    sel = jnp.zeros(gate.shape, jnp.bool_)
    for _ in range(MOBA_TOPK):
        m = jnp.max(g, axis=-1, keepdims=True)
        idx = jnp.min(jnp.where(g == m, lane_f, 1e9), axis=-1, keepdims=True)
        pick = lane_f == idx
        sel = jnp.logical_or(sel, pick)
        g = jnp.where(pick, -jnp.inf, g)
    return sel


def _attn_prompt_kernel(q_ref, k_ref, v_ref, o_ref, kaug_ref, vm_ref):
    seq = k_ref.shape[0]
    n_blocks = seq // MOBA_BLOCK
    lane1 = lax.broadcasted_iota(jnp.int32, (1, 128), 1)
    lane_s = lax.broadcasted_iota(jnp.int32, (seq, 128), 1)
    blk_s = lax.broadcasted_iota(jnp.int32, (seq, 128), 0) // MOBA_BLOCK
    k2 = k_ref[...]
    v2 = v_ref[...]
    kmean = jnp.sum(k2.reshape(n_blocks, MOBA_BLOCK, 128), axis=1) * (1.0 / MOBA_BLOCK)

    bias_base = (HEAD_DIM, 0)
    gate_w = []
    for h in range(2):
        own = jnp.logical_and(lane_s >= h * HEAD_DIM, lane_s < (h + 1) * HEAD_DIM)
        onehot = (lane_s == bias_base[h] + blk_s).astype(F32)
        kaug_ref[h] = jnp.where(own, k2, onehot).astype(BF16)
        vm_ref[h] = jnp.where(own, v2, 0.0).astype(BF16)
        own1 = jnp.logical_and(lane1 >= h * HEAD_DIM, lane1 < (h + 1) * HEAD_DIM)
        km = jnp.where(own1, kmean, 0.0)
        rows = [km, jnp.zeros((128 - n_blocks - bias_base[h], 128), F32)]
        if bias_base[h]:
            rows = [jnp.zeros((bias_base[h], 128), F32)] + rows
        gate_w.append(jnp.concatenate(rows, axis=0))

    lane_q = lax.broadcasted_iota(jnp.int32, (MOBA_BLOCK, 128), 1)
    lane_qf = lane_q.astype(F32)
    r_i = lax.broadcasted_iota(jnp.int32, (MOBA_BLOCK, MOBA_BLOCK), 0)
    c_i = lax.broadcasted_iota(jnp.int32, (MOBA_BLOCK, MOBA_BLOCK), 1)
    causal = c_i <= r_i

    for n in range(n_blocks):
        past = n * MOBA_BLOCK
        q2 = q_ref[past:past + MOBA_BLOCK, :]
        out = jnp.zeros((MOBA_BLOCK, 128), F32)
        for h in range(2):
            own = jnp.logical_and(lane_q >= h * HEAD_DIM, lane_q < (h + 1) * HEAD_DIM)
            qa = jnp.where(own, q2, 0.0)
            if n > MOBA_TOPK:
                gate = lax.dot_general(qa, gate_w[h], _NT, precision=lax.Precision.HIGHEST,
                                       preferred_element_type=F32)
                cand = jnp.logical_and(lane_q >= bias_base[h], lane_q < bias_base[h] + n)
                sel = _top3_mask(gate, cand, lane_qf)
                qa = qa + jnp.where(jnp.logical_and(cand, jnp.logical_not(sel)), MASK_BIAS, 0.0)
            qa = qa.astype(BF16)
            s_own = lax.dot_general(qa, kaug_ref[h, past:past + MOBA_BLOCK, :], _NT,
                                    preferred_element_type=F32)
            s_own = jnp.where(causal, s_own, -jnp.inf)
            m = jnp.max(s_own, axis=-1, keepdims=True)
            if n:
                s_past = lax.dot_general(qa, kaug_ref[h, 0:past, :], _NT,
                                         preferred_element_type=F32)
                m = jnp.maximum(m, jnp.max(s_past, axis=-1, keepdims=True))
            p_own = jnp.exp(s_own - m)
            l = jnp.sum(p_own, axis=-1, keepdims=True)
            o = jnp.dot(p_own.astype(BF16), vm_ref[h, past:past + MOBA_BLOCK, :],
                        preferred_element_type=F32)
            if n:
                p_past = jnp.exp(s_past - m)
                l = l + jnp.sum(p_past, axis=-1, keepdims=True)
                o = o + jnp.dot(p_past.astype(BF16), vm_ref[h, 0:past, :],
                                preferred_element_type=F32)
            out = out + o / l
        o_ref[past:past + MOBA_BLOCK, :] = out


def _attn_prompt(q, k, v, seq, name):
    n = q.shape[0]
    n_pairs = ATTN_WIDTH // 128
    spec = pl.BlockSpec((seq, 128), lambda b, hp: (b, hp))
    return pl.pallas_call(
        _attn_prompt_kernel,
        grid=(n // seq, n_pairs),
        in_specs=[spec, spec, spec],
        out_specs=spec,
        out_shape=jax.ShapeDtypeStruct((n, ATTN_WIDTH), F32),
        scratch_shapes=[pltpu.VMEM((2, seq, 128), BF16), pltpu.VMEM((2, seq, 128), BF16)],
        compiler_params=pltpu.CompilerParams(
            dimension_semantics=("arbitrary", "arbitrary"), vmem_limit_bytes=VMEM_LIMIT),
        name=name,
    )(q, k, v)


def _attn_sample_kernel(pt_ref, q_ref, kn_ref, vn_ref, *refs, n_pages, page):
    del pt_ref
    k_pages = refs[:n_pages]
    v_pages = refs[n_pages:2 * n_pages]
    o_ref = refs[2 * n_pages]
    n_new = q_ref.shape[1]
    pages_per_block = MOBA_BLOCK // page
    n_blocks = n_pages // pages_per_block
    n_rows = n_new * N_HEADS

    lane = lax.broadcasted_iota(jnp.int32, (N_HEADS, ATTN_WIDTH), 1)
    head = lax.broadcasted_iota(jnp.int32, (N_HEADS, ATTN_WIDTH), 0)
    head_lanes = (lane // HEAD_DIM == head).astype(F32)
    q_rows = [q_ref[0, t:t + 1, :] * head_lanes for t in range(n_new)]
    qbd = jnp.concatenate(q_rows, axis=0)
    qbd_b = qbd.astype(BF16)
    row_t = lax.broadcasted_iota(jnp.int32, (n_rows, 1), 0) // N_HEADS

    lane_w = lax.broadcasted_iota(jnp.int32, (ATTN_WIDTH, 128), 1)
    gate_w = jnp.zeros((ATTN_WIDTH, 128), F32)
    scores = []
    for j in range(n_blocks):
        kj = jnp.concatenate([k_pages[pages_per_block * j + i][...]
                              for i in range(pages_per_block)], axis=1)
        mean_j = jnp.sum(kj, axis=1, keepdims=True) * (1.0 / MOBA_BLOCK)
        gate_w = jnp.where(lane_w == j, mean_j, gate_w)
        scores.append(jnp.dot(qbd_b, kj.astype(BF16), preferred_element_type=F32))

    gate = jnp.dot(qbd, gate_w, precision=lax.Precision.HIGHEST,
                   preferred_element_type=F32)
    lane_g = lax.broadcasted_iota(jnp.int32, (n_rows, 128), 1)
    lane_gf = lane_g.astype(F32)
    sel = _top3_mask(gate, lane_g < n_blocks, lane_gf).astype(F32)

    m = jnp.full((n_rows, 1), -jnp.inf, F32)
    for j in range(n_blocks):
        sel_j = jnp.sum(jnp.where(lane_g == j, sel, 0.0), axis=-1, keepdims=True) > 0.5
        scores[j] = jnp.where(sel_j, scores[j], -jnp.inf)
        m = jnp.maximum(m, jnp.max(scores[j], axis=-1, keepdims=True))
    s_new = []
    for t in range(n_new):
        s = jnp.sum(qbd * kn_ref[0, t:t + 1, :], axis=-1, keepdims=True)
        s = jnp.where(row_t >= t, s, -jnp.inf)
        s_new.append(s)
        m = jnp.maximum(m, s)

    l = jnp.zeros((n_rows, 1), F32)
    acc = jnp.zeros((n_rows, ATTN_WIDTH), F32)
    for j in range(n_blocks):
        p = jnp.exp(scores[j] - m)
        l = l + jnp.sum(p, axis=-1, keepdims=True)
        vj = jnp.concatenate([v_pages[pages_per_block * j + i][...]
                              for i in range(pages_per_block)], axis=1)
        acc = acc + lax.dot_general(p.astype(BF16), vj.astype(BF16), _NT,
                                    preferred_element_type=F32)
    for t in range(n_new):
        p = jnp.exp(s_new[t] - m)
        l = l + p
        acc = acc + p * vn_ref[0, t:t + 1, :]

    acc = acc / l * jnp.concatenate([head_lanes] * n_new, axis=0)
    o_ref[0] = jnp.sum(acc.reshape(n_new, N_HEADS, ATTN_WIDTH), axis=1)


def _attn_sample(q, k_new, v_new, cache_k, cache_v, page_table, layer, name):
    n_seq, n_new, _ = q.shape
    n_pages = page_table.shape[1]
    page = cache_k.shape[3]
    new_spec = pl.BlockSpec((1, n_new, ATTN_WIDTH), lambda a, pt: (a, 0, 0))

    def page_spec(j):
        return pl.BlockSpec((None, None, ATTN_WIDTH, page),
                            lambda a, pt: (layer, pt[a, j], 0, 0))

    grid_spec = pltpu.PrefetchScalarGridSpec(
        num_scalar_prefetch=1,
        grid=(n_seq,),
        in_specs=[new_spec] * 3 + [page_spec(j) for j in range(n_pages)] * 2,
        out_specs=new_spec,
    )
    return pl.pallas_call(
        functools.partial(_attn_sample_kernel, n_pages=n_pages, page=page),
        grid_spec=grid_spec,
        out_shape=jax.ShapeDtypeStruct((n_seq, n_new, ATTN_WIDTH), F32),
        compiler_params=pltpu.CompilerParams(
            dimension_semantics=("arbitrary",), vmem_limit_bytes=VMEM_LIMIT),
        name=name,
    )(page_table, q, k_new, v_new, *([cache_k] * n_pages), *([cache_v] * n_pages))


def _pool_windows(read, tm, count):
    groups = []
    for g, w in enumerate(POOL_WINDOWS):
        cur = read(0, g)
        total = cur
        for back in range(1, w):
            total = total + read(back, g)
        groups.append(total / count(w) - cur)
    return groups


def _mix_even_kernel(*refs, sample, seq_tiles):
    n_groups = len(POOL_WINDOWS)
    if sample:
        attn_ref = refs[0]
        p_refs = refs[1:1 + n_groups]
        (hist_ref, y_ref, wpool_ref, pscale_ref, wo_ref, gffn_ref, wup_ref, wdn_ref,
         o_ref, pooled_ref) = refs[1 + n_groups:]
        tm = y_ref.shape[0]
    else:
        (attn_ref, p_ref, halo_ref, y_ref, wpool_ref, pscale_ref, wo_ref,
         gffn_ref, wup_ref, wdn_ref, o_ref, xp_ref) = refs
        tm = p_ref.shape[0]

    if sample:
        n_seq = hist_ref.shape[1]
        n_new = tm // n_seq
        for t in range(n_new):
            def read(back, g, t=t):
                if back <= t:
                    return p_refs[g][pl.ds(t - back, n_seq, stride=n_new), :]
                return hist_ref[POOL_HIST + t - back, :, g * POOL_GROUP:(g + 1) * POOL_GROUP]
            groups = _pool_windows(read, n_seq, lambda w: float(w))
            for g in range(n_groups):
                pooled_ref[g, pl.ds(t, n_seq, stride=n_new), :] = groups[g]
        groups = [pooled_ref[g] for g in range(n_groups)]
    else:
        seq_pos0 = (pl.program_id(0) % seq_tiles) * tm
        halo = halo_ref[...]
        xp_ref[0:HALO, :] = jnp.where(seq_pos0 == 0, 0.0, halo)
        xp_ref[HALO:, :] = p_ref[...]
        pos1 = (seq_pos0 + lax.broadcasted_iota(jnp.int32, (tm, 1), 0) + 1).astype(F32)

        def read(back, g):
            return xp_ref[pl.ds(HALO - back, tm), g * POOL_GROUP:(g + 1) * POOL_GROUP]
        groups = _pool_windows(read, tm, lambda w: jnp.minimum(pos1, float(w)))

    pooled = [jnp.dot(groups[g].astype(BF16), wpool_ref[g], preferred_element_type=F32)
              for g in range(len(POOL_WINDOWS))]
    pooled = jnp.concatenate(pooled, axis=1) * pscale_ref[...]
    mix = jnp.concatenate([attn_ref[...], pooled], axis=1).astype(BF16)
    y = y_ref[...] + jnp.dot(mix, wo_ref[...], preferred_element_type=F32)
    o_ref[...] = _ffn(y, gffn_ref, wup_ref, wdn_ref)


def _mix_even(attn, p, hist, y, wpool, pscale, wo, gffn, wup, wdn, seq, name):
    n = y.shape[0]
    tm = ROW_TILE
    sample = hist is not None
    row = lambda i: (i, 0)
    n_groups = len(POOL_WINDOWS)
    if sample:
        assert n == tm
        group_view = lambda g: pl.BlockSpec((tm, POOL_GROUP), lambda i: (0, g))
        pool_args = [p] * n_groups + [hist]
        pool_specs = [group_view(g) for g in range(n_groups)] + [_const_spec(hist.shape)]
        scratch = pltpu.VMEM((n_groups, tm, POOL_GROUP), F32)
        seq_tiles = 1
    else:
        assert seq % tm == 0
        blocks_per_tile = tm // HALO
        pool_args = [p, p]
        pool_specs = [
            pl.BlockSpec((tm, POOL_WIDTH), row),
            pl.BlockSpec((HALO, POOL_WIDTH),
                         lambda i: (jnp.maximum(i * blocks_per_tile - 1, 0), 0)),
        ]
        scratch = pltpu.VMEM((tm + HALO, POOL_WIDTH), F32)
        seq_tiles = seq // tm
    return pl.pallas_call(
        functools.partial(_mix_even_kernel, sample=sample, seq_tiles=seq_tiles),
        grid=(n // tm,),
        in_specs=[pl.BlockSpec((tm, ATTN_WIDTH), row)] + pool_specs + [
            pl.BlockSpec((tm, D_MODEL), row),
            _const_spec(wpool.shape),
            _const_spec((1, POOL_WIDTH)),
            _const_spec(wo.shape),
            _const_spec((1, D_MODEL)),
            _const_spec(wup.shape),
            _const_spec(wdn.shape),
        ],
        out_specs=pl.BlockSpec((tm, D_MODEL), row),
        out_shape=jax.ShapeDtypeStruct((n, D_MODEL), F32),
        scratch_shapes=[scratch],
        compiler_params=pltpu.CompilerParams(
            dimension_semantics=("arbitrary",), vmem_limit_bytes=VMEM_LIMIT),
        name=name,
    )(attn, *pool_args, y, wpool, pscale, wo, gffn, wup, wdn)


def _odd_kernel(*refs, emit_v, final_norm):
    (x_ref, gmix_ref, win_ref, lng_ref, lnb_ref, ws_ref, bs_ref, wo_ref,
     gffn_ref, wup_ref, wdn_ref) = refs[:11]
    rest = list(refs[11:])
    gfin_ref = rest.pop(0) if final_norm else None
    o_ref = rest.pop(0)
    v_ref = rest.pop(0) if emit_v else None
    tm = x_ref.shape[0]

    x = x_ref[...]
    xn = _rms(x, gmix_ref[...]).astype(BF16)
    z = jax.nn.gelu(jnp.dot(xn, win_ref[...], preferred_element_type=F32))
    u = z[:, :D_MODEL]
    v = z[:, D_MODEL:]
    mu = jnp.mean(v, axis=-1, keepdims=True)
    vc = v - mu
    var = jnp.mean(vc * vc, axis=-1, keepdims=True)
    v = vc * lax.rsqrt(var + LN_EPS) * lng_ref[...] + lnb_ref[...]
    if emit_v:
        v_ref[...] = v
    vb = v.astype(BF16)
    bias = bs_ref[...]
    chunks = []
    for c in range(tm // SGU_CHUNK):
        rows = slice(c * SGU_CHUNK, (c + 1) * SGU_CHUNK)
        cols = [jnp.dot(ws_ref[g], vb[rows, g * 128:(g + 1) * 128],
                        preferred_element_type=F32) for g in range(N_SGU_GROUPS)]
        chunks.append(jnp.concatenate(cols, axis=1) + bias)
    gated = (u * jnp.concatenate(chunks, axis=0)).astype(BF16)
    y = x + jnp.dot(gated, wo_ref[...], preferred_element_type=F32)
    y = _ffn(y, gffn_ref, wup_ref, wdn_ref)
    if final_norm:
        y = _rms(y, gfin_ref[...])
    o_ref[...] = y


def _odd_layer(x, gmix, win, lng, lnb, ws, bs, wo, gffn, wup, wdn, gfin, emit_v, name):
    n = x.shape[0]
    tm = ROW_TILE
    row = lambda i: (i, 0)
    final_norm = gfin is not None
    args = [x, gmix, win, lng, lnb, ws, bs, wo, gffn, wup, wdn]
    in_specs = [pl.BlockSpec((tm, D_MODEL), row)] + [_const_spec(a.shape) for a in args[1:]]
    if final_norm:
        args.append(gfin)
        in_specs.append(_const_spec(gfin.shape))
    out_specs = [pl.BlockSpec((tm, D_MODEL), row)]
    out_shape = [jax.ShapeDtypeStruct((n, D_MODEL), F32)]
    if emit_v:
        out_specs.append(pl.BlockSpec((tm, D_MODEL), row))
        out_shape.append(jax.ShapeDtypeStruct((n, D_MODEL), F32))
    outs = pl.pallas_call(
        functools.partial(_odd_kernel, emit_v=emit_v, final_norm=final_norm),
        grid=(n // tm,),
        in_specs=in_specs,
        out_specs=out_specs,
        out_shape=out_shape,
        compiler_params=pltpu.CompilerParams(
            dimension_semantics=("arbitrary",), vmem_limit_bytes=VMEM_LIMIT),
        name=name,
    )(*args)
    return outs if emit_v else (outs[0], None)


def _rope_tables(pos):
    inv = ROPE_THETA ** (-jnp.arange(0, HEAD_DIM, 2, dtype=F32) / HEAD_DIM)
    ang = pos.astype(F32)[:, None] * inv[None, :]
    cos, sin = jnp.cos(ang), jnp.sin(ang)
    return (jnp.concatenate([cos, cos, cos, cos], axis=1),
            jnp.concatenate([-sin, sin, -sin, sin], axis=1))


def _sgu_weights(w_s, b_s, length):
    tril = jnp.tril(jnp.ones((length, length), dtype=bool))
    w = jnp.where(tril, w_s[:, :length, :length], 0.0)
    reps = SGU_CHUNK // length
    if reps > 1:
        eye = jnp.eye(reps, dtype=w.dtype)
        w = jnp.einsum('ab,gts->gatbs', eye, w).reshape(
            N_SGU_GROUPS, SGU_CHUNK, SGU_CHUNK)
    b = jnp.tile(b_s[:, :length], (1, reps))
    b = jnp.repeat(b.T, D_MODEL // N_SGU_GROUPS, axis=1)
    return w.astype(BF16), b


def kernel(x_prompt, x_sample, cache_k, cache_v, state_pool, page_table, norm_mix, w_in_even, w_o_even, w_pool, pool_scale, w_in_odd, sgu_ln_g, sgu_ln_b, sgu_w, sgu_b, w_o_odd, norm_ffn, w_ffn_up, w_ffn_down, norm_final):
    batch, seq, _ = x_prompt.shape
    n_seq, n_new, _ = x_sample.shape
    depth = norm_mix.shape[0]
    n_phys, page = cache_k.shape[1], cache_k.shape[2]
    past_len = page_table.shape[1] * page

    yp = x_prompt.reshape(batch * seq, D_MODEL)
    ys = x_sample.reshape(n_seq * n_new, D_MODEL)
    ck = jnp.transpose(cache_k, (0, 1, 3, 4, 2)).reshape(cache_k.shape[0], n_phys, ATTN_WIDTH, page)
    cv = jnp.transpose(cache_v, (0, 1, 3, 4, 2)).reshape(cache_v.shape[0], n_phys, ATTN_WIDTH, page)

    cos_p, sin_p = _rope_tables(jnp.arange(seq))
    cos_s, sin_s = _rope_tables(past_len + jnp.arange(n_seq * n_new) % n_new)

    w_in_even_b = w_in_even.astype(BF16)
    w_o_even_b = w_o_even.astype(BF16)
    w_pool_b = w_pool.astype(BF16)
    w_in_odd_b = w_in_odd.astype(BF16)
    w_o_odd_b = w_o_odd.astype(BF16)
    w_up_b = w_ffn_up.astype(BF16)
    w_dn_b = w_ffn_down.astype(BF16)

    prompt_k, prompt_v, sample_k, sample_v = [], [], [], []
    prompt_pool, sample_pool, sample_sgu = [], [], []
    for l in range(depth):
        gmix = norm_mix[l][None, :]
        gffn = norm_ffn[l][None, :]
        if l % 2 == 0:
            i = l // 2
            pscale = pool_scale[i][None, :]
            q, k, v, p = _inproj_even(yp, gmix, w_in_even_b[i], cos_p, sin_p, f"inproj_p{l}")
            attn = _attn_prompt(q, k, v, seq, f"attn_p{l}")
            yp = _mix_even(attn, p, None, yp, w_pool_b[i], pscale, w_o_even_b[i], gffn,
                           w_up_b[l], w_dn_b[l], seq, f"mix_p{l}")
            prompt_k.append(k.reshape(batch, seq, N_HEADS, HEAD_DIM))
            prompt_v.append(v.reshape(batch, seq, N_HEADS, HEAD_DIM))
            prompt_pool.append(p.reshape(batch, seq, POOL_WIDTH)[:, seq - POOL_HIST:, :])

            q, k, v, p = _inproj_even(ys, gmix, w_in_even_b[i], cos_s, sin_s, f"inproj_s{l}")
            shape3 = (n_seq, n_new, ATTN_WIDTH)
            attn = _attn_sample(q.reshape(shape3), k.reshape(shape3), v.reshape(shape3),
                                ck, cv, page_table, i, f"attn_s{l}")
            hist = jnp.transpose(state_pool[i], (1, 0, 2))
            ys = _mix_even(attn.reshape(n_seq * n_new, ATTN_WIDTH), p, hist, ys, w_pool_b[i],
                           pscale, w_o_even_b[i], gffn, w_up_b[l], w_dn_b[l], n_new,
                           f"mix_s{l}")
            sample_k.append(k.reshape(n_seq, n_new, N_HEADS, HEAD_DIM))
            sample_v.append(v.reshape(n_seq, n_new, N_HEADS, HEAD_DIM))
            sample_pool.append(jnp.concatenate(
                [state_pool[i], p.reshape(n_seq, n_new, POOL_WIDTH)], axis=1)[:, -POOL_HIST:, :])
        else:
            j = l // 2
            gfin = norm_final[None, :] if l == depth - 1 else None
            lng = sgu_ln_g[j][None, :]
            lnb = sgu_ln_b[j][None, :]
            ws_p, bs_p = _sgu_weights(sgu_w[j], sgu_b[j], SGU_CHUNK)
            ws_s, bs_s = _sgu_weights(sgu_w[j], sgu_b[j], n_new)
            yp, _ = _odd_layer(yp, gmix, w_in_odd_b[j], lng, lnb, ws_p, bs_p, w_o_odd_b[j],
                               gffn, w_up_b[l], w_dn_b[l], gfin, False, f"odd_p{l}")
            ys, v_rows = _odd_layer(ys, gmix, w_in_odd_b[j], lng, lnb, ws_s, bs_s, w_o_odd_b[j],
                                    gffn, w_up_b[l], w_dn_b[l], gfin, True, f"odd_s{l}")
            sample_sgu.append(v_rows.reshape(n_seq, n_new, D_MODEL))
    if depth % 2 == 1:
        raise NotImplementedError("the final norm is fused into an odd last layer")
    return (yp.reshape(batch, seq, D_MODEL), ys.reshape(n_seq, n_new, D_MODEL),
            jnp.stack(prompt_k), jnp.stack(prompt_v),
            jnp.stack(sample_k), jnp.stack(sample_v),
            jnp.stack(prompt_pool), jnp.stack(sample_pool),
            jnp.stack(sample_sgu))
```

```python
import functools

import jax
import jax.numpy as jnp
from jax import lax
from jax.experimental import pallas as pl
from jax.experimental.pallas import tpu as pltpu

F32 = jnp.float32
BF16 = jnp.bfloat16

D_MODEL = 1024
HEAD_DIM = 64
N_HEADS = 8
ATTN_WIDTH = 512
POOL_WIDTH = 512
POOL_WINDOWS = (2, 4, 8, 16)
POOL_GROUP = 128
POOL_HIST = 15
MOBA_BLOCK = 256
MOBA_TOPK = 3
ROPE_THETA = 10000.0
SGU_CHUNK = 128
N_SGU_GROUPS = 8
D_FF = 4096
NORM_EPS = 1e-6
LN_EPS = 1e-5

ROW_TILE = 512
FF_CHUNK = 1024
HALO = 16
MASK_BIAS = -1e30
Q_SCALE = HEAD_DIM ** -0.5 * 1.4426950408889634
V_ROWS = HEAD_DIM + 16
VMEM_LIMIT = 56 * 1024 * 1024

_NT = (((1,), (1,)), ((), ()))


def _rms(x, g):
    ms = jnp.mean(x * x, axis=-1, keepdims=True)
    return x * lax.rsqrt(ms + NORM_EPS) * g


def _ffn(y, g_ref, wup_ref, wdn_ref):
    hn = _rms(y, g_ref[...]).astype(BF16)
    acc = y
    for c in range(D_FF // FF_CHUNK):
        sl = slice(c * FF_CHUNK, (c + 1) * FF_CHUNK)
        h = jnp.dot(hn, wup_ref[:, sl], preferred_element_type=F32)
        h = jnp.maximum(h, 0.0)
        h = (h * h).astype(BF16)
        acc = acc + jnp.dot(h, wdn_ref[sl, :], preferred_element_type=F32)
    return acc


def _const_spec(shape):
    nd = len(shape)
    return pl.BlockSpec(shape, lambda *_: (0,) * nd, pipeline_mode=pl.Buffered(1))


def _inproj_even_kernel(x_ref, g_ref, w_ref, cos_ref, sin_ref, *refs, transposed):
    if transposed:
        q_ref, k_ref, kt_ref, vt_ref, p_ref = refs[-5:]
    else:
        q_ref, k_ref, v_ref, p_ref = refs
    xn = _rms(x_ref[...], g_ref[...]).astype(BF16)
    proj = jnp.dot(xn, w_ref[...], preferred_element_type=F32)
    cos = jnp.concatenate([cos_ref[...]] * 4, axis=1)
    sin = jnp.concatenate([sin_ref[...]] * 4, axis=1)
    lane = lax.broadcasted_iota(jnp.int32, cos.shape, 1) % HEAD_DIM
    first_half = lane < HEAD_DIM // 2

    def rot(a):
        partner = jnp.where(first_half,
                            pltpu.roll(a, ATTN_WIDTH - HEAD_DIM // 2, 1),
                            pltpu.roll(a, HEAD_DIM // 2, 1))
        return a * cos + partner * sin

    q_ref[...] = rot(proj[:, 0:ATTN_WIDTH]) * Q_SCALE
    k = rot(proj[:, ATTN_WIDTH:2 * ATTN_WIDTH])
    v = proj[:, 2 * ATTN_WIDTH:3 * ATTN_WIDTH]
    k_ref[...] = k
    if transposed:
        kt_ref[...] = k.T
        vt_ref[...] = v.T
    else:
        v_ref[...] = v
    p_ref[...] = proj[:, 3 * ATTN_WIDTH:]


def _inproj_even(x, g, w, cos, sin, name, kv_t_shape=None, kv_t=None, layer=0):
    n = x.shape[0]
    tm = ROW_TILE
    n_pos_blocks = cos.shape[0] // tm
    row = lambda i: (i, 0)
    out = jax.ShapeDtypeStruct((n, ATTN_WIDTH), F32)
    row_spec = pl.BlockSpec((tm, ATTN_WIDTH), row)
    in_specs = [
        pl.BlockSpec((tm, D_MODEL), row),
        _const_spec((1, D_MODEL)),
        _const_spec(w.shape),
        pl.BlockSpec((tm, 128), lambda i: (i % n_pos_blocks, 0)),
        pl.BlockSpec((tm, 128), lambda i: (i % n_pos_blocks, 0)),
    ]
    args = [x, g, w, cos, sin]
    aliases = {}
    if kv_t_shape is None:
        out_specs, out_shape = [row_spec] * 4, [out] * 4
    else:
        seq_tiles = kv_t_shape[3] // tm
        t_spec = pl.BlockSpec((None, None, ATTN_WIDTH, tm),
                              lambda i: (layer, i // seq_tiles, 0, i % seq_tiles))
        t_shape = jax.ShapeDtypeStruct(kv_t_shape, F32)
        if kv_t is not None:
            aliases = {len(args): 2, len(args) + 1: 3}
            in_specs += [pl.BlockSpec(memory_space=pl.ANY)] * 2
            args += list(kv_t)
        out_specs = [row_spec, row_spec, t_spec, t_spec, row_spec]
        out_shape = [out, out, t_shape, t_shape, out]
    return pl.pallas_call(
        functools.partial(_inproj_even_kernel, transposed=kv_t_shape is not None),
        grid=(n // tm,),
        in_specs=in_specs,
        out_specs=out_specs,
        out_shape=out_shape,
        input_output_aliases=aliases,
        compiler_params=pltpu.CompilerParams(
            dimension_semantics=("arbitrary",), vmem_limit_bytes=VMEM_LIMIT),
        name=name,
    )(*args)


def _top3_mask(gate, cand, index_f, axis=-1):
    g = jnp.where(cand, gate, -jnp.inf)
    sel = jnp.zeros(gate.shape, jnp.bool_)
    for _ in range(MOBA_TOPK):
        m = jnp.max(g, axis=axis, keepdims=True)
        idx = jnp.min(jnp.where(g == m, index_f, 1e9), axis=axis, keepdims=True)
        pick = index_f == idx
        sel = jnp.logical_or(sel, pick)
        g = jnp.where(pick, -jnp.inf, g)
    return sel


def _attn_prompt_kernel(q_ref, k_ref, vt_ref, o_ref, kh_ref, vtb_ref):
    seq = k_ref.shape[0]
    n_blocks = seq // MOBA_BLOCK
    lane1 = lax.broadcasted_iota(jnp.int32, (1, 128), 1)
    lane_s = lax.broadcasted_iota(jnp.int32, (seq, 128), 1)
    k2 = k_ref[...]
    kmean = jnp.sum(k2.reshape(n_blocks, MOBA_BLOCK, 128), axis=1) * (1.0 / MOBA_BLOCK)
    gate_w = []
    for h in range(2):
        vtb_ref[h, 0:HEAD_DIM, :] = vt_ref[h * HEAD_DIM:(h + 1) * HEAD_DIM, :].astype(BF16)
        vtb_ref[h, HEAD_DIM:, :] = jnp.ones((V_ROWS - HEAD_DIM, seq), BF16)
        own = jnp.logical_and(lane_s >= h * HEAD_DIM, lane_s < (h + 1) * HEAD_DIM)
        kh_ref[h] = jnp.where(own, k2, 0.0).astype(BF16)
        own1 = jnp.logical_and(lane1 >= h * HEAD_DIM, lane1 < (h + 1) * HEAD_DIM)
        gate_w.append(jnp.where(own1, kmean, 0.0))

    blk_i = lax.broadcasted_iota(jnp.int32, (n_blocks, MOBA_BLOCK), 0)
    blk_f = blk_i.astype(F32)
    key_i = lax.broadcasted_iota(jnp.int32, (MOBA_BLOCK, MOBA_BLOCK), 0)
    qry_i = lax.broadcasted_iota(jnp.int32, (MOBA_BLOCK, MOBA_BLOCK), 1)
    causal = key_i <= qry_i

    def blk(j):
        return slice(j * MOBA_BLOCK, (j + 1) * MOBA_BLOCK)

    def scores_pass(n, h, st):
        q2 = q_ref[blk(n), :]
        q2b = q2.astype(BF16)
        bias = None
        if n > MOBA_TOPK:
            gate = lax.dot_general(gate_w[h], q2, _NT, precision=lax.Precision.HIGHEST,
                                   preferred_element_type=F32)
            cand = blk_i < n
            sel = _top3_mask(gate, cand, blk_f, axis=0)
            bias = jnp.where(jnp.logical_and(cand, jnp.logical_not(sel)), MASK_BIAS, 0.0)
        m = None
        s_blocks = []
        s_t = lax.dot_general(kh_ref[h, 0:(n + 1) * MOBA_BLOCK, :], q2b, _NT,
                              preferred_element_type=F32)
        for j in range(n + 1):
            s = s_t[blk(j)]
            if j == n:
                s = jnp.where(causal, s, -jnp.inf)
            mj = jnp.max(s.reshape(MOBA_BLOCK // 8, 8, MOBA_BLOCK), axis=0)
            if bias is not None and j < n:
                mj = mj + bias[j:j + 1]
            m = mj if m is None else jnp.maximum(m, mj)
            s_blocks.append(s)
            yield
        m = jnp.max(m, axis=0, keepdims=True)
        st["s"] = s_blocks
        st["shift"] = [m if (bias is None or j == n) else m - bias[j:j + 1]
                       for j in range(n + 1)]

    def output_pass(n, h, st):
        o_t = None
        for j in range(n + 1):
            p = jnp.exp2(st["s"][j] - st["shift"][j]).astype(BF16)
            oj = jnp.dot(vtb_ref[h, :, blk(j)], p, preferred_element_type=F32)
            o_t = oj if o_t is None else o_t + oj
            yield
        o_ref[h * HEAD_DIM:(h + 1) * HEAD_DIM, blk(n)] = (
            o_t[0:HEAD_DIM] / o_t[HEAD_DIM:HEAD_DIM + 1])

    units = [(n, h) for n in range(n_blocks) for h in range(2)]
    states = [dict() for _ in units]
    for _ in scores_pass(*units[0], states[0]):
        pass
    for u, unit in enumerate(units):
        gens = [output_pass(*unit, states[u])]
        if u + 1 < len(units):
            gens.append(scores_pass(*units[u + 1], states[u + 1]))
        while gens:
            for g in list(gens):
                if next(g, StopIteration) is StopIteration:
                    gens.remove(g)


def _attn_prompt(q, k, vt_all, layer, seq, name):
    n = q.shape[0]
    batch = n // seq
    n_pairs = ATTN_WIDTH // 128
    row_spec = pl.BlockSpec((seq, 128), lambda b, hp: (b, hp))
    return pl.pallas_call(
        _attn_prompt_kernel,
        grid=(batch, n_pairs),
        in_specs=[row_spec, row_spec,
                  pl.BlockSpec((None, None, 128, seq), lambda b, hp: (layer, b, hp, 0))],
        out_specs=pl.BlockSpec((None, 128, seq), lambda b, hp: (b, hp, 0)),
        out_shape=jax.ShapeDtypeStruct((batch, ATTN_WIDTH, seq), F32),
        scratch_shapes=[pltpu.VMEM((2, seq, 128), BF16), pltpu.VMEM((2, V_ROWS, seq), BF16)],
        compiler_params=pltpu.CompilerParams(
            dimension_semantics=("arbitrary", "arbitrary"), vmem_limit_bytes=VMEM_LIMIT),
        name=name,
    )(q, k, vt_all)


def _attn_sample_kernel(pt_ref, q_ref, kn_ref, vn_ref, *refs, n_pages, page):
    del pt_ref
    k_pages = refs[:n_pages]
    v_pages = refs[n_pages:2 * n_pages]
    o_ref = refs[2 * n_pages]
    n_new = q_ref.shape[1]
    pages_per_block = MOBA_BLOCK // page
    n_blocks = n_pages // pages_per_block
    n_rows = n_new * N_HEADS

    lane = lax.broadcasted_iota(jnp.int32, (N_HEADS, ATTN_WIDTH), 1)
    head = lax.broadcasted_iota(jnp.int32, (N_HEADS, ATTN_WIDTH), 0)
    head_lanes = (lane // HEAD_DIM == head).astype(F32)
    q_rows = [q_ref[0, t:t + 1, :] * head_lanes for t in range(n_new)]
    qbd = jnp.concatenate(q_rows, axis=0)
    qbd_b = qbd.astype(BF16)
    row_t = lax.broadcasted_iota(jnp.int32, (n_rows, 1), 0) // N_HEADS

    lane_w = lax.broadcasted_iota(jnp.int32, (ATTN_WIDTH, 128), 1)
    gate_w = jnp.zeros((ATTN_WIDTH, 128), F32)
    scores = []
    for j in range(n_blocks):
        kj = jnp.concatenate([k_pages[pages_per_block * j + i][...]
                              for i in range(pages_per_block)], axis=1)
        mean_j = jnp.sum(kj, axis=1, keepdims=True) * (1.0 / MOBA_BLOCK)
        gate_w = jnp.where(lane_w == j, mean_j, gate_w)
        scores.append(jnp.dot(qbd_b, kj.astype(BF16), preferred_element_type=F32))

    gate = jnp.dot(qbd, gate_w, precision=lax.Precision.HIGHEST,
                   preferred_element_type=F32)
    lane_g = lax.broadcasted_iota(jnp.int32, (n_rows, 128), 1)
    lane_gf = lane_g.astype(F32)
    sel = _top3_mask(gate, lane_g < n_blocks, lane_gf).astype(F32)

    m = jnp.full((n_rows, 1), -jnp.inf, F32)
    for j in range(n_blocks):
        sel_j = jnp.sum(jnp.where(lane_g == j, sel, 0.0), axis=-1, keepdims=True) > 0.5
        scores[j] = jnp.where(sel_j, scores[j], -jnp.inf)
        m = jnp.maximum(m, jnp.max(scores[j], axis=-1, keepdims=True))
    s_new = []
    for t in range(n_new):
        s = jnp.sum(qbd * kn_ref[0, t:t + 1, :], axis=-1, keepdims=True)
        s = jnp.where(row_t >= t, s, -jnp.inf)
        s_new.append(s)
        m = jnp.maximum(m, s)

    l = jnp.zeros((n_rows, 1), F32)
    acc = jnp.zeros((n_rows, ATTN_WIDTH), F32)
    for j in range(n_blocks):
        p = jnp.exp2(scores[j] - m)
        l = l + jnp.sum(p, axis=-1, keepdims=True)
        vj = jnp.concatenate([v_pages[pages_per_block * j + i][...]
                              for i in range(pages_per_block)], axis=1)
        acc = acc + lax.dot_general(p.astype(BF16), vj.astype(BF16), _NT,
                                    preferred_element_type=F32)
    for t in range(n_new):
        p = jnp.exp2(s_new[t] - m)
        l = l + p
        acc = acc + p * vn_ref[0, t:t + 1, :]

    acc = acc / l * jnp.concatenate([head_lanes] * n_new, axis=0)
    o_ref[0] = jnp.sum(acc.reshape(n_new, N_HEADS, ATTN_WIDTH), axis=1)


def _attn_sample(q, k_new, v_new, cache_k, cache_v, page_table, layer, name):
    n_seq, n_new, _ = q.shape
    n_pages = page_table.shape[1]
    page = cache_k.shape[3]
    new_spec = pl.BlockSpec((1, n_new, ATTN_WIDTH), lambda a, pt: (a, 0, 0))

    def page_spec(j):
        return pl.BlockSpec((None, None, ATTN_WIDTH, page),
                            lambda a, pt: (layer, pt[a, j], 0, 0))

    grid_spec = pltpu.PrefetchScalarGridSpec(
        num_scalar_prefetch=1,
        grid=(n_seq,),
        in_specs=[new_spec] * 3 + [page_spec(j) for j in range(n_pages)] * 2,
        out_specs=new_spec,
    )
    return pl.pallas_call(
        functools.partial(_attn_sample_kernel, n_pages=n_pages, page=page),
        grid_spec=grid_spec,
        out_shape=jax.ShapeDtypeStruct((n_seq, n_new, ATTN_WIDTH), F32),
        compiler_params=pltpu.CompilerParams(
            dimension_semantics=("arbitrary",), vmem_limit_bytes=VMEM_LIMIT),
        name=name,
    )(page_table, q, k_new, v_new, *([cache_k] * n_pages), *([cache_v] * n_pages))


def _pool_windows(read, tm, count):
    groups = []
    for g, w in enumerate(POOL_WINDOWS):
        cur = read(0, g)
        total = cur
        for back in range(1, w):
            total = total + read(back, g)
        groups.append(total / count(w) - cur)
    return groups


def _mix_even_kernel(*refs, sample, seq_tiles):
    n_groups = len(POOL_WINDOWS)
    if sample:
        attn_ref = refs[0]
        p_refs = refs[1:1 + n_groups]
        (hist_ref, y_ref, wpool_ref, pscale_ref, wo_ref, gffn_ref, wup_ref, wdn_ref,
         o_ref, pooled_ref) = refs[1 + n_groups:]
        tm = y_ref.shape[0]
    else:
        (attn_ref, p_ref, halo_ref, y_ref, wpool_ref, pscale_ref, wo_ref,
         gffn_ref, wup_ref, wdn_ref, o_ref, xp_ref) = refs
        tm = p_ref.shape[0]

    if sample:
        n_seq = hist_ref.shape[1]
        n_new = tm // n_seq
        for t in range(n_new):
            def read(back, g, t=t):
                if back <= t:
                    return p_refs[g][pl.ds(t - back, n_seq, stride=n_new), :]
                return hist_ref[POOL_HIST + t - back, :, g * POOL_GROUP:(g + 1) * POOL_GROUP]
            groups = _pool_windows(read, n_seq, lambda w: float(w))
            for g in range(n_groups):
                pooled_ref[g, pl.ds(t, n_seq, stride=n_new), :] = groups[g]
        groups = [pooled_ref[g] for g in range(n_groups)]
    else:
        seq_pos0 = (pl.program_id(0) % seq_tiles) * tm
        halo = halo_ref[...]
        xp_ref[0:HALO, :] = jnp.where(seq_pos0 == 0, 0.0, halo)
        xp_ref[HALO:, :] = p_ref[...]
        pos1 = (seq_pos0 + lax.broadcasted_iota(jnp.int32, (tm, 1), 0) + 1).astype(F32)

        def read(back, g):
            return xp_ref[pl.ds(HALO - back, tm), g * POOL_GROUP:(g + 1) * POOL_GROUP]
        groups = _pool_windows(read, tm, lambda w: jnp.minimum(pos1, float(w)))

    pooled = [jnp.dot(groups[g].astype(BF16), wpool_ref[g], preferred_element_type=F32)
              for g in range(len(POOL_WINDOWS))]
    pooled = jnp.concatenate(pooled, axis=1) * pscale_ref[...]
    attn = attn_ref[...] if sample else attn_ref[...].T
    mix = jnp.concatenate([attn, pooled], axis=1).astype(BF16)
    y = y_ref[...] + jnp.dot(mix, wo_ref[...], preferred_element_type=F32)
    o_ref[...] = _ffn(y, gffn_ref, wup_ref, wdn_ref)


def _mix_even(attn, p, hist, y, wpool, pscale, wo, gffn, wup, wdn, seq, name):
    n = y.shape[0]
    tm = ROW_TILE
    sample = hist is not None
    row = lambda i: (i, 0)
    n_groups = len(POOL_WINDOWS)
    if sample:
        assert n == tm
        group_view = lambda g: pl.BlockSpec((tm, POOL_GROUP), lambda i: (0, g))
        pool_args = [p] * n_groups + [hist]
        pool_specs = [group_view(g) for g in range(n_groups)] + [_const_spec(hist.shape)]
        scratch = pltpu.VMEM((n_groups, tm, POOL_GROUP), F32)
        seq_tiles = 1
        attn_spec = pl.BlockSpec((tm, ATTN_WIDTH), row)
    else:
        assert seq % tm == 0
        attn_spec = pl.BlockSpec((None, ATTN_WIDTH, tm),
                                 lambda i: (i // (seq // tm), 0, i % (seq // tm)))
        blocks_per_tile = tm // HALO
        pool_args = [p, p]
        pool_specs = [
            pl.BlockSpec((tm, POOL_WIDTH), row),
            pl.BlockSpec((HALO, POOL_WIDTH),
                         lambda i: (jnp.maximum(i * blocks_per_tile - 1, 0), 0)),
        ]
        scratch = pltpu.VMEM((tm + HALO, POOL_WIDTH), F32)
        seq_tiles = seq // tm
    return pl.pallas_call(
        functools.partial(_mix_even_kernel, sample=sample, seq_tiles=seq_tiles),
        grid=(n // tm,),
        in_specs=[attn_spec] + pool_specs + [
            pl.BlockSpec((tm, D_MODEL), row),
            _const_spec(wpool.shape),
            _const_spec((1, POOL_WIDTH)),
            _const_spec(wo.shape),
            _const_spec((1, D_MODEL)),
            _const_spec(wup.shape),
            _const_spec(wdn.shape),
        ],
        out_specs=pl.BlockSpec((tm, D_MODEL), row),
        out_shape=jax.ShapeDtypeStruct((n, D_MODEL), F32),
        scratch_shapes=[scratch],
        compiler_params=pltpu.CompilerParams(
            dimension_semantics=("arbitrary",), vmem_limit_bytes=VMEM_LIMIT),
        name=name,
    )(attn, *pool_args, y, wpool, pscale, wo, gffn, wup, wdn)


def _odd_kernel(*refs, emit_v, final_norm):
    (x_ref, gmix_ref, win_ref, lng_ref, lnb_ref, ws_ref, bs_ref, wo_ref,
     gffn_ref, wup_ref, wdn_ref) = refs[:11]
    rest = list(refs[11:])
    gfin_ref = rest.pop(0) if final_norm else None
    o_ref = rest.pop(0)
    v_ref = rest.pop(0) if emit_v else None
    tm = x_ref.shape[0]

    x = x_ref[...]
    xn = _rms(x, gmix_ref[...]).astype(BF16)
    z = jax.nn.gelu(jnp.dot(xn, win_ref[...], preferred_element_type=F32))
    u = z[:, :D_MODEL]
    v = z[:, D_MODEL:]
    mu = jnp.mean(v, axis=-1, keepdims=True)
    vc = v - mu
    var = jnp.mean(vc * vc, axis=-1, keepdims=True)
    v = vc * lax.rsqrt(var + LN_EPS) * lng_ref[...] + lnb_ref[...]
    if emit_v:
        v_ref[...] = v
    vb = v.astype(BF16)
    bias = bs_ref[...]
    chunks = []
    for c in range(tm // SGU_CHUNK):
        rows = slice(c * SGU_CHUNK, (c + 1) * SGU_CHUNK)
        cols = [jnp.dot(ws_ref[g], vb[rows, g * 128:(g + 1) * 128],
                        preferred_element_type=F32) for g in range(N_SGU_GROUPS)]
        chunks.append(jnp.concatenate(cols, axis=1) + bias)
    gated = (u * jnp.concatenate(chunks, axis=0)).astype(BF16)
    y = x + jnp.dot(gated, wo_ref[...], preferred_element_type=F32)
    y = _ffn(y, gffn_ref, wup_ref, wdn_ref)
    if final_norm:
        y = _rms(y, gfin_ref[...])
    o_ref[...] = y


def _odd_layer(x, gmix, win, lng, lnb, ws, bs, wo, gffn, wup, wdn, gfin, emit_v, name):
    n = x.shape[0]
    tm = ROW_TILE
    row = lambda i: (i, 0)
    final_norm = gfin is not None
    args = [x, gmix, win, lng, lnb, ws, bs, wo, gffn, wup, wdn]
    in_specs = [pl.BlockSpec((tm, D_MODEL), row)] + [_const_spec(a.shape) for a in args[1:]]
    if final_norm:
        args.append(gfin)
        in_specs.append(_const_spec(gfin.shape))
    out_specs = [pl.BlockSpec((tm, D_MODEL), row)]
    out_shape = [jax.ShapeDtypeStruct((n, D_MODEL), F32)]
    if emit_v:
        out_specs.append(pl.BlockSpec((tm, D_MODEL), row))
        out_shape.append(jax.ShapeDtypeStruct((n, D_MODEL), F32))
    outs = pl.pallas_call(
        functools.partial(_odd_kernel, emit_v=emit_v, final_norm=final_norm),
        grid=(n // tm,),
        in_specs=in_specs,
        out_specs=out_specs,
        out_shape=out_shape,
        compiler_params=pltpu.CompilerParams(
            dimension_semantics=("arbitrary",), vmem_limit_bytes=VMEM_LIMIT),
        name=name,
    )(*args)
    return outs if emit_v else (outs[0], None)


def _rope_tables(pos):
    inv = ROPE_THETA ** (-jnp.arange(0, HEAD_DIM, 2, dtype=F32) / HEAD_DIM)
    ang = pos.astype(F32)[:, None] * inv[None, :]
    cos, sin = jnp.cos(ang), jnp.sin(ang)
    return (jnp.concatenate([cos, cos, cos, cos], axis=1),
            jnp.concatenate([-sin, sin, -sin, sin], axis=1))


def _sgu_weights(w_s, b_s, length):
    tril = jnp.tril(jnp.ones((length, length), dtype=bool))
    w = jnp.where(tril, w_s[:, :length, :length], 0.0)
    reps = SGU_CHUNK // length
    if reps > 1:
        eye = jnp.eye(reps, dtype=w.dtype)
        w = jnp.einsum('ab,gts->gatbs', eye, w).reshape(
            N_SGU_GROUPS, SGU_CHUNK, SGU_CHUNK)
    b = jnp.tile(b_s[:, :length], (1, reps))
    b = jnp.repeat(b.T, D_MODEL // N_SGU_GROUPS, axis=1)
    return w.astype(BF16), b


def kernel(x_prompt, x_sample, cache_k, cache_v, state_pool, page_table, norm_mix, w_in_even, w_o_even, w_pool, pool_scale, w_in_odd, sgu_ln_g, sgu_ln_b, sgu_w, sgu_b, w_o_odd, norm_ffn, w_ffn_up, w_ffn_down, norm_final):
    batch, seq, _ = x_prompt.shape
    n_seq, n_new, _ = x_sample.shape
    depth = norm_mix.shape[0]
    n_phys, page = cache_k.shape[1], cache_k.shape[2]
    past_len = page_table.shape[1] * page

    yp = x_prompt.reshape(batch * seq, D_MODEL)
    ys = x_sample.reshape(n_seq * n_new, D_MODEL)
    ck = jnp.transpose(cache_k, (0, 1, 3, 4, 2)).reshape(cache_k.shape[0], n_phys, ATTN_WIDTH, page)
    cv = jnp.transpose(cache_v, (0, 1, 3, 4, 2)).reshape(cache_v.shape[0], n_phys, ATTN_WIDTH, page)

    cos_p, sin_p = _rope_tables(jnp.arange(seq))
    cos_s, sin_s = _rope_tables(past_len + jnp.arange(n_seq * n_new) % n_new)

    w_in_even_b = w_in_even.astype(BF16)
    w_o_even_b = w_o_even.astype(BF16)
    w_pool_b = w_pool.astype(BF16)
    w_in_odd_b = w_in_odd.astype(BF16)
    w_o_odd_b = w_o_odd.astype(BF16)
    w_up_b = w_ffn_up.astype(BF16)
    w_dn_b = w_ffn_down.astype(BF16)

    sample_k, sample_v = [], []
    prompt_pool, sample_pool, sample_sgu = [], [], []
    n_even = (depth + 1) // 2
    kv_t = None
    for l in range(depth):
        gmix = norm_mix[l][None, :]
        gffn = norm_ffn[l][None, :]
        if l % 2 == 0:
            i = l // 2
            pscale = pool_scale[i][None, :]
            q, k, kt_all, vt_all, p = _inproj_even(
                yp, gmix, w_in_even_b[i], cos_p, sin_p, f"inproj_p{l}",
                kv_t_shape=(n_even, batch, ATTN_WIDTH, seq), kv_t=kv_t, layer=i)
            kv_t = (kt_all, vt_all)
            attn = _attn_prompt(q, k, vt_all, i, seq, f"attn_p{l}")
            yp = _mix_even(attn, p, None, yp, w_pool_b[i], pscale, w_o_even_b[i], gffn,
                           w_up_b[l], w_dn_b[l], seq, f"mix_p{l}")
            prompt_pool.append(p.reshape(batch, seq, POOL_WIDTH)[:, seq - POOL_HIST:, :])

            q, k, v, p = _inproj_even(ys, gmix, w_in_even_b[i], cos_s, sin_s, f"inproj_s{l}")
            shape3 = (n_seq, n_new, ATTN_WIDTH)
            attn = _attn_sample(q.reshape(shape3), k.reshape(shape3), v.reshape(shape3),
                                ck, cv, page_table, i, f"attn_s{l}")
            hist = jnp.transpose(state_pool[i], (1, 0, 2))
            ys = _mix_even(attn.reshape(n_seq * n_new, ATTN_WIDTH), p, hist, ys, w_pool_b[i],
                           pscale, w_o_even_b[i], gffn, w_up_b[l], w_dn_b[l], n_new,
                           f"mix_s{l}")
            sample_k.append(k.reshape(n_seq, n_new, N_HEADS, HEAD_DIM))
            sample_v.append(v.reshape(n_seq, n_new, N_HEADS, HEAD_DIM))
            sample_pool.append(jnp.concatenate(
                [state_pool[i], p.reshape(n_seq, n_new, POOL_WIDTH)], axis=1)[:, -POOL_HIST:, :])
        else:
            j = l // 2
            gfin = norm_final[None, :] if l == depth - 1 else None
            lng = sgu_ln_g[j][None, :]
            lnb = sgu_ln_b[j][None, :]
            ws_p, bs_p = _sgu_weights(sgu_w[j], sgu_b[j], SGU_CHUNK)
            ws_s, bs_s = _sgu_weights(sgu_w[j], sgu_b[j], n_new)
            yp, _ = _odd_layer(yp, gmix, w_in_odd_b[j], lng, lnb, ws_p, bs_p, w_o_odd_b[j],
                               gffn, w_up_b[l], w_dn_b[l], gfin, False, f"odd_p{l}")
            ys, v_rows = _odd_layer(ys, gmix, w_in_odd_b[j], lng, lnb, ws_s, bs_s, w_o_odd_b[j],
                                    gffn, w_up_b[l], w_dn_b[l], gfin, True, f"odd_s{l}")
            sample_sgu.append(v_rows.reshape(n_seq, n_new, D_MODEL))
    if depth % 2 == 1:
        raise NotImplementedError("the final norm is fused into an odd last layer")
    prompt_k, prompt_v = [
        jnp.transpose(t.reshape(n_even, batch, N_HEADS, HEAD_DIM, seq), (0, 1, 4, 2, 3))
        for t in kv_t]
    return (yp.reshape(batch, seq, D_MODEL), ys.reshape(n_seq, n_new, D_MODEL),
            prompt_k, prompt_v,
            jnp.stack(sample_k), jnp.stack(sample_v),
            jnp.stack(prompt_pool), jnp.stack(sample_pool),
            jnp.stack(sample_sgu))
```

```python
import functools

import jax
import jax.numpy as jnp
from jax import lax
from jax.experimental import pallas as pl
from jax.experimental.pallas import tpu as pltpu

F32 = jnp.float32
BF16 = jnp.bfloat16

D_MODEL = 1024
HEAD_DIM = 64
N_HEADS = 8
ATTN_WIDTH = 512
POOL_WIDTH = 512
POOL_WINDOWS = (2, 4, 8, 16)
POOL_GROUP = 128
POOL_HIST = 15
MOBA_BLOCK = 256
MOBA_TOPK = 3
ROPE_THETA = 10000.0
SGU_CHUNK = 128
N_SGU_GROUPS = 8
D_FF = 4096
NORM_EPS = 1e-6
LN_EPS = 1e-5

ROW_TILE = 1024
MIX_TILE = 512
FF_CHUNK = 1024
HALO = 16
MASK_BIAS = -1e30
Q_SCALE = HEAD_DIM ** -0.5 * 1.4426950408889634
V_ROWS = HEAD_DIM + 16
SAMPLE_GROUP = 2
VMEM_LIMIT = 56 * 1024 * 1024

_NT = (((1,), (1,)), ((), ()))


def _rms(x, g):
    ms = jnp.mean(x * x, axis=-1, keepdims=True)
    return x * lax.rsqrt(ms + NORM_EPS) * g


def _ffn(y, g_ref, wup_ref, wdn_ref):
    hn = _rms(y, g_ref[...]).astype(BF16)
    acc = y
    for c in range(D_FF // FF_CHUNK):
        sl = slice(c * FF_CHUNK, (c + 1) * FF_CHUNK)
        h = jnp.dot(hn, wup_ref[:, sl], preferred_element_type=F32)
        h = jnp.maximum(h, 0.0)
        h = (h * h).astype(BF16)
        acc = acc + jnp.dot(h, wdn_ref[sl, :], preferred_element_type=F32)
    return acc


def _const_spec(shape):
    nd = len(shape)
    return pl.BlockSpec(shape, lambda *_: (0,) * nd, pipeline_mode=pl.Buffered(1))


def _param_spec(a):
    if isinstance(a, tuple):
        w, layer = a
        nd = w.ndim - 1
        return pl.BlockSpec((None,) + w.shape[1:], lambda *_: (layer,) + (0,) * nd,
                            pipeline_mode=pl.Buffered(1))
    return _const_spec(a.shape)


def _param_arg(a):
    return a[0] if isinstance(a, tuple) else a


def _inproj_even_kernel(x_ref, g_ref, w_ref, cos_ref, sin_ref, *refs, transposed):
    if transposed:
        q_ref, k_ref, kt_ref, vt_ref, p_ref = refs[-5:]
    else:
        q_ref, k_ref, v_ref, p_ref = refs
    xn = _rms(x_ref[...], g_ref[...]).astype(BF16)
    proj = jnp.dot(xn, w_ref[...], preferred_element_type=F32)
    cos = jnp.concatenate([cos_ref[...]] * 4, axis=1)
    sin = jnp.concatenate([sin_ref[...]] * 4, axis=1)
    lane = lax.broadcasted_iota(jnp.int32, cos.shape, 1) % HEAD_DIM
    first_half = lane < HEAD_DIM // 2

    def rot(a):
        partner = jnp.where(first_half,
                            pltpu.roll(a, ATTN_WIDTH - HEAD_DIM // 2, 1),
                            pltpu.roll(a, HEAD_DIM // 2, 1))
        return a * cos + partner * sin

    q_ref[...] = rot(proj[:, 0:ATTN_WIDTH]) * Q_SCALE
    k = rot(proj[:, ATTN_WIDTH:2 * ATTN_WIDTH])
    v = proj[:, 2 * ATTN_WIDTH:3 * ATTN_WIDTH]
    k_ref[...] = k
    if transposed:
        kt_ref[...] = k.T
        vt_ref[...] = v.T
    else:
        v_ref[...] = v
    p_ref[...] = proj[:, 3 * ATTN_WIDTH:]


def _inproj_even(x, g, w, cos, sin, name, kv_t_shape=None, kv_t=None, layer=0):
    n = x.shape[0]
    tm = min(ROW_TILE, n)
    n_pos_blocks = cos.shape[0] // tm
    row = lambda i: (i, 0)
    out = jax.ShapeDtypeStruct((n, ATTN_WIDTH), F32)
    row_spec = pl.BlockSpec((tm, ATTN_WIDTH), row)
    in_specs = [
        pl.BlockSpec((tm, D_MODEL), row),
        _param_spec(g),
        _param_spec(w),
        pl.BlockSpec((tm, 128), lambda i: (i % n_pos_blocks, 0)),
        pl.BlockSpec((tm, 128), lambda i: (i % n_pos_blocks, 0)),
    ]
    args = [x, _param_arg(g), _param_arg(w), cos, sin]
    aliases = {}
    if kv_t_shape is None:
        out_specs, out_shape = [row_spec] * 4, [out] * 4
    else:
        seq_tiles = kv_t_shape[3] // tm
        t_spec = pl.BlockSpec((None, None, ATTN_WIDTH, tm),
                              lambda i: (layer, i // seq_tiles, 0, i % seq_tiles))
        t_shape = jax.ShapeDtypeStruct(kv_t_shape, F32)
        if kv_t is not None:
            aliases = {len(args): 2, len(args) + 1: 3}
            in_specs += [pl.BlockSpec(memory_space=pl.ANY)] * 2
            args += list(kv_t)
        out_specs = [row_spec, row_spec, t_spec, t_spec, row_spec]
        out_shape = [out, out, t_shape, t_shape, out]
    return pl.pallas_call(
        functools.partial(_inproj_even_kernel, transposed=kv_t_shape is not None),
        grid=(n // tm,),
        in_specs=in_specs,
        out_specs=out_specs,
        out_shape=out_shape,
        input_output_aliases=aliases,
        compiler_params=pltpu.CompilerParams(
            dimension_semantics=("arbitrary",), vmem_limit_bytes=VMEM_LIMIT),
        name=name,
    )(*args)


def _top3_mask(gate, cand, index_f, axis=-1):
    g = jnp.where(cand, gate, -jnp.inf)
    sel = jnp.zeros(gate.shape, jnp.bool_)
    for _ in range(MOBA_TOPK):
        m = jnp.max(g, axis=axis, keepdims=True)
        idx = jnp.min(jnp.where(g == m, index_f, 1e9), axis=axis, keepdims=True)
        pick = index_f == idx
        sel = jnp.logical_or(sel, pick)
        g = jnp.where(pick, -jnp.inf, g)
    return sel


def _attn_prompt_kernel(q_ref, k_ref, vt_ref, o_ref, kh_ref, vtb_ref):
    seq = k_ref.shape[0]
    n_blocks = seq // MOBA_BLOCK
    lane1 = lax.broadcasted_iota(jnp.int32, (1, 128), 1)
    lane_s = lax.broadcasted_iota(jnp.int32, (seq, 128), 1)
    k2 = k_ref[...]
    kmean = jnp.sum(k2.reshape(n_blocks, MOBA_BLOCK, 128), axis=1) * (1.0 / MOBA_BLOCK)
    gate_w = []
    for h in range(2):
        vtb_ref[h, 0:HEAD_DIM, :] = vt_ref[h * HEAD_DIM:(h + 1) * HEAD_DIM, :].astype(BF16)
        vtb_ref[h, HEAD_DIM:, :] = jnp.ones((V_ROWS - HEAD_DIM, seq), BF16)
        own = jnp.logical_and(lane_s >= h * HEAD_DIM, lane_s < (h + 1) * HEAD_DIM)
        kh_ref[h] = jnp.where(own, k2, 0.0).astype(BF16)
        own1 = jnp.logical_and(lane1 >= h * HEAD_DIM, lane1 < (h + 1) * HEAD_DIM)
        gate_w.append(jnp.where(own1, kmean, 0.0))

    blk_i = lax.broadcasted_iota(jnp.int32, (n_blocks, MOBA_BLOCK), 0)
    blk_f = blk_i.astype(F32)
    key_i = lax.broadcasted_iota(jnp.int32, (MOBA_BLOCK, MOBA_BLOCK), 0)
    qry_i = lax.broadcasted_iota(jnp.int32, (MOBA_BLOCK, MOBA_BLOCK), 1)
    causal = key_i <= qry_i

    def blk(j):
        return slice(j * MOBA_BLOCK, (j + 1) * MOBA_BLOCK)

    def scores_pass(n, h, st):
        q2 = q_ref[blk(n), :]
        q2b = q2.astype(BF16)
        bias = None
        if n > MOBA_TOPK:
            gate = lax.dot_general(gate_w[h], q2, _NT, precision=lax.Precision.HIGHEST,
                                   preferred_element_type=F32)
            cand = blk_i < n
            sel = _top3_mask(gate, cand, blk_f, axis=0)
            bias = jnp.where(jnp.logical_and(cand, jnp.logical_not(sel)), MASK_BIAS, 0.0)
        m = None
        s_blocks = []
        s_t = lax.dot_general(kh_ref[h, 0:(n + 1) * MOBA_BLOCK, :], q2b, _NT,
                              preferred_element_type=F32)
        for j in range(n + 1):
            s = s_t[blk(j)]
            if j == n:
                s = jnp.where(causal, s, -jnp.inf)
            mj = jnp.max(s.reshape(MOBA_BLOCK // 8, 8, MOBA_BLOCK), axis=0)
            if bias is not None and j < n:
                mj = mj + bias[j:j + 1]
            m = mj if m is None else jnp.maximum(m, mj)
            s_blocks.append(s)
            yield
        m = jnp.max(m, axis=0, keepdims=True)
        st["s"] = s_blocks
        st["shift"] = [m if (bias is None or j == n) else m - bias[j:j + 1]
                       for j in range(n + 1)]

    def output_pass(n, h, st):
        o_t = None
        for j in range(n + 1):
            p = jnp.exp2(st["s"][j] - st["shift"][j]).astype(BF16)
            oj = jnp.dot(vtb_ref[h, :, blk(j)], p, preferred_element_type=F32)
            o_t = oj if o_t is None else o_t + oj
            yield
        o_ref[h * HEAD_DIM:(h + 1) * HEAD_DIM, blk(n)] = (
            o_t[0:HEAD_DIM] / o_t[HEAD_DIM:HEAD_DIM + 1])

    units = [(n, h) for n in range(n_blocks) for h in range(2)]
    states = [dict() for _ in units]
    for _ in scores_pass(*units[0], states[0]):
        pass
    for u, unit in enumerate(units):
        gens = [output_pass(*unit, states[u])]
        if u + 1 < len(units):
            gens.append(scores_pass(*units[u + 1], states[u + 1]))
        while gens:
            for g in list(gens):
                if next(g, StopIteration) is StopIteration:
                    gens.remove(g)


def _attn_prompt(q, k, vt_all, layer, seq, name):
    n = q.shape[0]
    batch = n // seq
    n_pairs = ATTN_WIDTH // 128
    row_spec = pl.BlockSpec((seq, 128), lambda b, hp: (b, hp))
    return pl.pallas_call(
        _attn_prompt_kernel,
        grid=(batch, n_pairs),
        in_specs=[row_spec, row_spec,
                  pl.BlockSpec((None, None, 128, seq), lambda b, hp: (layer, b, hp, 0))],
        out_specs=pl.BlockSpec((None, 128, seq), lambda b, hp: (b, hp, 0)),
        out_shape=jax.ShapeDtypeStruct((batch, ATTN_WIDTH, seq), F32),
        scratch_shapes=[pltpu.VMEM((2, seq, 128), BF16), pltpu.VMEM((2, V_ROWS, seq), BF16)],
        compiler_params=pltpu.CompilerParams(
            dimension_semantics=("arbitrary", "arbitrary"), vmem_limit_bytes=VMEM_LIMIT),
        name=name,
    )(q, k, vt_all)


def _attn_sample_kernel(pt_ref, q_ref, kn_ref, vn_ref, *refs, n_pages, page, group):
    del pt_ref
    o_ref = refs[2 * group * n_pages]
    n_new = q_ref.shape[1]
    pages_per_block = MOBA_BLOCK // page
    n_blocks = n_pages // pages_per_block
    n_rows = n_new * N_HEADS

    lane = lax.broadcasted_iota(jnp.int32, (N_HEADS, ATTN_WIDTH), 1)
    head = lax.broadcasted_iota(jnp.int32, (N_HEADS, ATTN_WIDTH), 0)
    head_lanes = (lane // HEAD_DIM == head).astype(F32)
    row_t = lax.broadcasted_iota(jnp.int32, (n_rows, 1), 0) // N_HEADS
    lane_w = lax.broadcasted_iota(jnp.int32, (ATTN_WIDTH, 128), 1)
    lane_g = lax.broadcasted_iota(jnp.int32, (n_rows, 128), 1)
    lane_gf = lane_g.astype(F32)

    def one_sequence(s):
        k_pages = refs[s * n_pages:(s + 1) * n_pages]
        v_pages = refs[(group + s) * n_pages:(group + s + 1) * n_pages]

        def block_of(pages, j):
            return jnp.concatenate([pages[pages_per_block * j + i][...]
                                    for i in range(pages_per_block)], axis=1)

        q_rows = [q_ref[s, t:t + 1, :] * head_lanes for t in range(n_new)]
        qbd = jnp.concatenate(q_rows, axis=0)
        qbd_b = qbd.astype(BF16)
        gate_w = jnp.zeros((ATTN_WIDTH, 128), F32)
        scores = []
        for j in range(n_blocks):
            kj = block_of(k_pages, j)
            mean_j = jnp.sum(kj, axis=1, keepdims=True) * (1.0 / MOBA_BLOCK)
            gate_w = jnp.where(lane_w == j, mean_j, gate_w)
            scores.append(jnp.dot(qbd_b, kj.astype(BF16), preferred_element_type=F32))
            yield

        gate = jnp.dot(qbd, gate_w, precision=lax.Precision.HIGHEST,
                       preferred_element_type=F32)
        sel = _top3_mask(gate, lane_g < n_blocks, lane_gf).astype(F32)
        yield

        m = jnp.full((n_rows, 1), -jnp.inf, F32)
        for j in range(n_blocks):
            sel_j = jnp.sum(jnp.where(lane_g == j, sel, 0.0), axis=-1, keepdims=True) > 0.5
            scores[j] = jnp.where(sel_j, scores[j], -jnp.inf)
            m = jnp.maximum(m, jnp.max(scores[j], axis=-1, keepdims=True))
        s_new = []
        for t in range(n_new):
            sc = jnp.sum(qbd * kn_ref[s, t:t + 1, :], axis=-1, keepdims=True)
            sc = jnp.where(row_t >= t, sc, -jnp.inf)
            s_new.append(sc)
            m = jnp.maximum(m, sc)
        yield

        l = jnp.zeros((n_rows, 1), F32)
        acc = jnp.zeros((n_rows, ATTN_WIDTH), F32)
        for j in range(n_blocks):
            p = jnp.exp2(scores[j] - m)
            l = l + jnp.sum(p, axis=-1, keepdims=True)
            acc = acc + lax.dot_general(p.astype(BF16), block_of(v_pages, j).astype(BF16), _NT,
                                        preferred_element_type=F32)
            yield
        for t in range(n_new):
            p = jnp.exp2(s_new[t] - m)
            l = l + p
            acc = acc + p * vn_ref[s, t:t + 1, :]

        acc = acc / l * jnp.concatenate([head_lanes] * n_new, axis=0)
        o_ref[s] = jnp.sum(acc.reshape(n_new, N_HEADS, ATTN_WIDTH), axis=1)

    gens = [one_sequence(s) for s in range(group)]
    while gens:
        for g in list(gens):
            if next(g, StopIteration) is StopIteration:
                gens.remove(g)


def _attn_sample(q, k_new, v_new, cache_k, cache_v, page_table, layer, name):
    n_seq, n_new, _ = q.shape
    n_pages = page_table.shape[1]
    page = cache_k.shape[3]
    group = SAMPLE_GROUP
    new_spec = pl.BlockSpec((group, n_new, ATTN_WIDTH), lambda a, pt: (a, 0, 0))

    def page_spec(s, j):
        return pl.BlockSpec((None, None, ATTN_WIDTH, page),
                            lambda a, pt: (layer, pt[a * group + s, j], 0, 0))

    page_specs = [page_spec(s, j) for s in range(group) for j in range(n_pages)]
    grid_spec = pltpu.PrefetchScalarGridSpec(
        num_scalar_prefetch=1,
        grid=(n_seq // group,),
        in_specs=[new_spec] * 3 + page_specs * 2,
        out_specs=new_spec,
    )
    n_page_args = group * n_pages
    return pl.pallas_call(
        functools.partial(_attn_sample_kernel, n_pages=n_pages, page=page, group=group),
        grid_spec=grid_spec,
        out_shape=jax.ShapeDtypeStruct((n_seq, n_new, ATTN_WIDTH), F32),
        compiler_params=pltpu.CompilerParams(
            dimension_semantics=("arbitrary",), vmem_limit_bytes=VMEM_LIMIT),
        name=name,
    )(page_table, q, k_new, v_new, *([cache_k] * n_page_args), *([cache_v] * n_page_args))


def _pool_windows(read, tm, count):
    groups = []
    for g, w in enumerate(POOL_WINDOWS):
        cur = read(0, g)
        total = cur
        for back in range(1, w):
            total = total + read(back, g)
        groups.append(total / count(w) - cur)
    return groups


def _mix_even_kernel(*refs, sample, seq_tiles):
    n_groups = len(POOL_WINDOWS)
    if sample:
        attn_ref = refs[0]
        p_refs = refs[1:1 + n_groups]
        (hist_ref, y_ref, wpool_ref, pscale_ref, wo_ref, gffn_ref, wup_ref, wdn_ref,
         o_ref, pooled_ref) = refs[1 + n_groups:]
        tm = y_ref.shape[0]
    else:
        (attn_ref, p_ref, halo_ref, y_ref, wpool_ref, pscale_ref, wo_ref,
         gffn_ref, wup_ref, wdn_ref, o_ref, xp_ref) = refs
        tm = p_ref.shape[0]

    if sample:
        n_seq = hist_ref.shape[1]
        n_new = tm // n_seq
        for t in range(n_new):
            def read(back, g, t=t):
                if back <= t:
                    return p_refs[g][pl.ds(t - back, n_seq, stride=n_new), :]
                return hist_ref[POOL_HIST + t - back, :, g * POOL_GROUP:(g + 1) * POOL_GROUP]
            groups = _pool_windows(read, n_seq, lambda w: float(w))
            for g in range(n_groups):
                pooled_ref[g, pl.ds(t, n_seq, stride=n_new), :] = groups[g]
        groups = [pooled_ref[g] for g in range(n_groups)]
    else:
        seq_pos0 = (pl.program_id(0) % seq_tiles) * tm
        halo = halo_ref[...]
        xp_ref[0:HALO, :] = jnp.where(seq_pos0 == 0, 0.0, halo)
        xp_ref[HALO:, :] = p_ref[...]
        pos1 = (seq_pos0 + lax.broadcasted_iota(jnp.int32, (tm, 1), 0) + 1).astype(F32)

        def read(back, g):
            return xp_ref[pl.ds(HALO - back, tm), g * POOL_GROUP:(g + 1) * POOL_GROUP]
        groups = _pool_windows(read, tm, lambda w: jnp.minimum(pos1, float(w)))

    pooled = [jnp.dot(groups[g].astype(BF16), wpool_ref[g], preferred_element_type=F32)
              for g in range(len(POOL_WINDOWS))]
    pooled = jnp.concatenate(pooled, axis=1) * pscale_ref[...]
    attn = attn_ref[...] if sample else attn_ref[...].T
    mix = jnp.concatenate([attn, pooled], axis=1).astype(BF16)
    y = y_ref[...] + jnp.dot(mix, wo_ref[...], preferred_element_type=F32)
    o_ref[...] = _ffn(y, gffn_ref, wup_ref, wdn_ref)


def _mix_even(attn, p, hist, y, wpool, pscale, wo, gffn, wup, wdn, seq, name):
    n = y.shape[0]
    tm = min(MIX_TILE, n)
    sample = hist is not None
    row = lambda i: (i, 0)
    n_groups = len(POOL_WINDOWS)
    params = [wpool, pscale, wo, gffn, wup, wdn]
    if sample:
        assert n == tm
        group_view = lambda g: pl.BlockSpec((tm, POOL_GROUP), lambda i: (0, g))
        pool_args = [p] * n_groups + [hist]
        pool_specs = [group_view(g) for g in range(n_groups)] + [_const_spec(hist.shape)]
        scratch = pltpu.VMEM((n_groups, tm, POOL_GROUP), F32)
        seq_tiles = 1
        attn_spec = pl.BlockSpec((tm, ATTN_WIDTH), row)
    else:
        assert seq % tm == 0
        attn_spec = pl.BlockSpec((None, ATTN_WIDTH, tm),
                                 lambda i: (i // (seq // tm), 0, i % (seq // tm)))
        blocks_per_tile = tm // HALO
        pool_args = [p, p]
        pool_specs = [
            pl.BlockSpec((tm, POOL_WIDTH), row),
            pl.BlockSpec((HALO, POOL_WIDTH),
                         lambda i: (jnp.maximum(i * blocks_per_tile - 1, 0), 0)),
        ]
        scratch = pltpu.VMEM((tm + HALO, POOL_WIDTH), F32)
        seq_tiles = seq // tm
    return pl.pallas_call(
        functools.partial(_mix_even_kernel, sample=sample, seq_tiles=seq_tiles),
        grid=(n // tm,),
        in_specs=[attn_spec] + pool_specs + [
            pl.BlockSpec((tm, D_MODEL), row),
        ] + [_param_spec(a) for a in params],
        out_specs=pl.BlockSpec((tm, D_MODEL), row),
        out_shape=jax.ShapeDtypeStruct((n, D_MODEL), F32),
        scratch_shapes=[scratch],
        compiler_params=pltpu.CompilerParams(
            dimension_semantics=("arbitrary",), vmem_limit_bytes=VMEM_LIMIT),
        name=name,
    )(attn, *pool_args, y, *[_param_arg(a) for a in params])


def _odd_kernel(*refs, emit_v, final_norm):
    (x_ref, gmix_ref, win_ref, lng_ref, lnb_ref, ws_ref, bs_ref, wo_ref,
     gffn_ref, wup_ref, wdn_ref) = refs[:11]
    rest = list(refs[11:])
    gfin_ref = rest.pop(0) if final_norm else None
    o_ref = rest.pop(0)
    v_ref = rest.pop(0) if emit_v else None
    tm = x_ref.shape[0]

    x = x_ref[...]
    xn = _rms(x, gmix_ref[...]).astype(BF16)
    z = jax.nn.gelu(jnp.dot(xn, win_ref[...], preferred_element_type=F32))
    u = z[:, :D_MODEL]
    v = z[:, D_MODEL:]
    mu = jnp.mean(v, axis=-1, keepdims=True)
    vc = v - mu
    var = jnp.mean(vc * vc, axis=-1, keepdims=True)
    v = vc * lax.rsqrt(var + LN_EPS) * lng_ref[...] + lnb_ref[...]
    if emit_v:
        v_ref[...] = v
    vb = v.astype(BF16)
    bias = bs_ref[...]
    chunks = []
    for c in range(tm // SGU_CHUNK):
        rows = slice(c * SGU_CHUNK, (c + 1) * SGU_CHUNK)
        cols = [jnp.dot(ws_ref[g], vb[rows, g * 128:(g + 1) * 128],
                        preferred_element_type=F32) for g in range(N_SGU_GROUPS)]
        chunks.append(jnp.concatenate(cols, axis=1) + bias)
    gated = (u * jnp.concatenate(chunks, axis=0)).astype(BF16)
    y = x + jnp.dot(gated, wo_ref[...], preferred_element_type=F32)
    y = _ffn(y, gffn_ref, wup_ref, wdn_ref)
    if final_norm:
        y = _rms(y, gfin_ref[...])
    o_ref[...] = y


def _odd_layer(x, gmix, win, lng, lnb, ws, bs, wo, gffn, wup, wdn, gfin, emit_v, name):
    n = x.shape[0]
    tm = min(ROW_TILE, n)
    row = lambda i: (i, 0)
    final_norm = gfin is not None
    params = [gmix, win, lng, lnb, ws, bs, wo, gffn, wup, wdn]
    if final_norm:
        params.append(gfin)
    args = [x] + [_param_arg(a) for a in params]
    in_specs = [pl.BlockSpec((tm, D_MODEL), row)] + [_param_spec(a) for a in params]
    out_specs = [pl.BlockSpec((tm, D_MODEL), row)]
    out_shape = [jax.ShapeDtypeStruct((n, D_MODEL), F32)]
    if emit_v:
        out_specs.append(pl.BlockSpec((tm, D_MODEL), row))
        out_shape.append(jax.ShapeDtypeStruct((n, D_MODEL), F32))
    outs = pl.pallas_call(
        functools.partial(_odd_kernel, emit_v=emit_v, final_norm=final_norm),
        grid=(n // tm,),
        in_specs=in_specs,
        out_specs=out_specs,
        out_shape=out_shape,
        compiler_params=pltpu.CompilerParams(
            dimension_semantics=("arbitrary",), vmem_limit_bytes=VMEM_LIMIT),
        name=name,
    )(*args)
    return outs if emit_v else (outs[0], None)


def _rope_tables(pos):
    inv = ROPE_THETA ** (-jnp.arange(0, HEAD_DIM, 2, dtype=F32) / HEAD_DIM)
    ang = pos.astype(F32)[:, None] * inv[None, :]
    cos, sin = jnp.cos(ang), jnp.sin(ang)
    return (jnp.concatenate([cos, cos, cos, cos], axis=1),
            jnp.concatenate([-sin, sin, -sin, sin], axis=1))


def _sgu_weights(w_s, b_s, length):
    tril = jnp.tril(jnp.ones((length, length), dtype=bool))
    w = jnp.where(tril, w_s[:, :length, :length], 0.0)
    reps = SGU_CHUNK // length
    if reps > 1:
        eye = jnp.eye(reps, dtype=w.dtype)
        w = jnp.einsum('ab,gts->gatbs', eye, w).reshape(
            N_SGU_GROUPS, SGU_CHUNK, SGU_CHUNK)
    b = jnp.tile(b_s[:, :length], (1, reps))
    b = jnp.repeat(b.T, D_MODEL // N_SGU_GROUPS, axis=1)
    return w.astype(BF16), b


def kernel(x_prompt, x_sample, cache_k, cache_v, state_pool, page_table, norm_mix, w_in_even, w_o_even, w_pool, pool_scale, w_in_odd, sgu_ln_g, sgu_ln_b, sgu_w, sgu_b, w_o_odd, norm_ffn, w_ffn_up, w_ffn_down, norm_final):
    batch, seq, _ = x_prompt.shape
    n_seq, n_new, _ = x_sample.shape
    depth = norm_mix.shape[0]
    n_phys, page = cache_k.shape[1], cache_k.shape[2]
    past_len = page_table.shape[1] * page

    yp = x_prompt.reshape(batch * seq, D_MODEL)
    ys = x_sample.reshape(n_seq * n_new, D_MODEL)
    ck = jnp.transpose(cache_k, (0, 1, 3, 4, 2)).reshape(cache_k.shape[0], n_phys, ATTN_WIDTH, page)
    cv = jnp.transpose(cache_v, (0, 1, 3, 4, 2)).reshape(cache_v.shape[0], n_phys, ATTN_WIDTH, page)

    cos_p, sin_p = _rope_tables(jnp.arange(seq))
    cos_s, sin_s = _rope_tables(past_len + jnp.arange(n_seq * n_new) % n_new)

    w_in_even_b = w_in_even.astype(BF16)
    w_o_even_b = w_o_even.astype(BF16)
    w_pool_b = w_pool.astype(BF16)
    w_in_odd_b = w_in_odd.astype(BF16)
    w_o_odd_b = w_o_odd.astype(BF16)
    w_up_b = w_ffn_up.astype(BF16)
    w_dn_b = w_ffn_down.astype(BF16)

    sample_k, sample_v = [], []
    prompt_pool, sample_pool, sample_sgu = [], [], []
    n_even = (depth + 1) // 2
    kv_t = None
    for l in range(depth):
        gmix = (norm_mix[:, None, :], l)
        gffn = (norm_ffn[:, None, :], l)
        w_up_l, w_dn_l = (w_up_b, l), (w_dn_b, l)
        if l % 2 == 0:
            i = l // 2
            pscale = (pool_scale[:, None, :], i)
            w_in_l, w_o_l, w_pool_l = (w_in_even_b, i), (w_o_even_b, i), (w_pool_b, i)
            q, k, kt_all, vt_all, p = _inproj_even(
                yp, gmix, w_in_l, cos_p, sin_p, f"inproj_p{l}",
                kv_t_shape=(n_even, batch, ATTN_WIDTH, seq), kv_t=kv_t, layer=i)
            kv_t = (kt_all, vt_all)
            attn = _attn_prompt(q, k, vt_all, i, seq, f"attn_p{l}")
            yp = _mix_even(attn, p, None, yp, w_pool_l, pscale, w_o_l, gffn,
                           w_up_l, w_dn_l, seq, f"mix_p{l}")
            prompt_pool.append(p.reshape(batch, seq, POOL_WIDTH)[:, seq - POOL_HIST:, :])

            q, k, v, p = _inproj_even(ys, gmix, w_in_l, cos_s, sin_s, f"inproj_s{l}")
            shape3 = (n_seq, n_new, ATTN_WIDTH)
            attn = _attn_sample(q.reshape(shape3), k.reshape(shape3), v.reshape(shape3),
                                ck, cv, page_table, i, f"attn_s{l}")
            hist = jnp.transpose(state_pool[i], (1, 0, 2))
            ys = _mix_even(attn.reshape(n_seq * n_new, ATTN_WIDTH), p, hist, ys, w_pool_l,
                           pscale, w_o_l, gffn, w_up_l, w_dn_l, n_new,
                           f"mix_s{l}")
            sample_k.append(k.reshape(n_seq, n_new, N_HEADS, HEAD_DIM))
            sample_v.append(v.reshape(n_seq, n_new, N_HEADS, HEAD_DIM))
            sample_pool.append(jnp.concatenate(
                [state_pool[i], p.reshape(n_seq, n_new, POOL_WIDTH)], axis=1)[:, -POOL_HIST:, :])
        else:
            j = l // 2
            gfin = norm_final[None, :] if l == depth - 1 else None
            lng = (sgu_ln_g[:, None, :], j)
            lnb = (sgu_ln_b[:, None, :], j)
            w_in_l, w_o_l = (w_in_odd_b, j), (w_o_odd_b, j)
            ws_p, bs_p = _sgu_weights(sgu_w[j], sgu_b[j], SGU_CHUNK)
            ws_s, bs_s = _sgu_weights(sgu_w[j], sgu_b[j], n_new)
            yp, _ = _odd_layer(yp, gmix, w_in_l, lng, lnb, ws_p, bs_p, w_o_l,
                               gffn, w_up_l, w_dn_l, gfin, False, f"odd_p{l}")
            ys, v_rows = _odd_layer(ys, gmix, w_in_l, lng, lnb, ws_s, bs_s, w_o_l,
                                    gffn, w_up_l, w_dn_l, gfin, True, f"odd_s{l}")
            sample_sgu.append(v_rows.reshape(n_seq, n_new, D_MODEL))
    if depth % 2 == 1:
        raise NotImplementedError("the final norm is fused into an odd last layer")
    prompt_k, prompt_v = [
        jnp.transpose(t.reshape(n_even, batch, N_HEADS, HEAD_DIM, seq), (0, 1, 4, 2, 3))
        for t in kv_t]
    return (yp.reshape(batch, seq, D_MODEL), ys.reshape(n_seq, n_new, D_MODEL),
            prompt_k, prompt_v,
            jnp.stack(sample_k), jnp.stack(sample_v),
            jnp.stack(prompt_pool), jnp.stack(sample_pool),
            jnp.stack(sample_sgu))
```

```python
import functools

import jax
import jax.numpy as jnp
from jax import lax
from jax.experimental import pallas as pl
from jax.experimental.pallas import tpu as pltpu

F32 = jnp.float32
BF16 = jnp.bfloat16

D_MODEL = 1024
HEAD_DIM = 64
N_HEADS = 8
ATTN_WIDTH = 512
POOL_WIDTH = 512
POOL_WINDOWS = (2, 4, 8, 16)
POOL_GROUP = 128
POOL_HIST = 15
MOBA_BLOCK = 256
MOBA_TOPK = 3
ROPE_THETA = 10000.0
SGU_CHUNK = 128
N_SGU_GROUPS = 8
D_FF = 4096
NORM_EPS = 1e-6
LN_EPS = 1e-5

ROW_TILE = 1024
MIX_TILE = 512
FF_CHUNK = 1024
HALO = 16
MASK_BIAS = -1e30
Q_SCALE = HEAD_DIM ** -0.5 * 1.4426950408889634
V_ROWS = HEAD_DIM + 16
SAMPLE_GROUP = 2
ATTN_LEAD_UNITS = 2
VMEM_LIMIT = 56 * 1024 * 1024

_NT = (((1,), (1,)), ((), ()))


def _rms(x, g):
    ms = jnp.mean(x * x, axis=-1, keepdims=True)
    return x * lax.rsqrt(ms + NORM_EPS) * g


def _ffn(y, g_ref, wup_ref, wdn_ref):
    hn = _rms(y, g_ref[...]).astype(BF16)
    acc = y
    for c in range(D_FF // FF_CHUNK):
        sl = slice(c * FF_CHUNK, (c + 1) * FF_CHUNK)
        h = jnp.dot(hn, wup_ref[:, sl], preferred_element_type=F32)
        h = jnp.maximum(h, 0.0)
        h = (h * h).astype(BF16)
        acc = acc + jnp.dot(h, wdn_ref[sl, :], preferred_element_type=F32)
    return acc


def _round_robin(gens):
    gens = list(gens)
    while gens:
        for g in list(gens):
            if next(g, StopIteration) is StopIteration:
                gens.remove(g)


def _const_spec(shape):
    nd = len(shape)
    return pl.BlockSpec(shape, lambda *_: (0,) * nd, pipeline_mode=pl.Buffered(1))


def _param_spec(a):
    if isinstance(a, tuple):
        w, layer = a
        nd = w.ndim - 1
        return pl.BlockSpec((None,) + w.shape[1:], lambda *_: (layer,) + (0,) * nd,
                            pipeline_mode=pl.Buffered(1))
    return _const_spec(a.shape)


def _param_arg(a):
    return a[0] if isinstance(a, tuple) else a


def _inproj_even_kernel(x_ref, g_ref, w_ref, cos_ref, sin_ref, *refs, transposed):
    if transposed:
        q_ref, k_ref, kt_ref, vt_ref, p_ref = refs[-5:]
    else:
        q_ref, k_ref, v_ref, p_ref = refs
    xn = _rms(x_ref[...], g_ref[...]).astype(BF16)
    proj = jnp.dot(xn, w_ref[...], preferred_element_type=F32)
    cos = jnp.concatenate([cos_ref[...]] * 4, axis=1)
    sin = jnp.concatenate([sin_ref[...]] * 4, axis=1)
    lane = lax.broadcasted_iota(jnp.int32, cos.shape, 1) % HEAD_DIM
    first_half = lane < HEAD_DIM // 2

    def rot(a):
        partner = jnp.where(first_half,
                            pltpu.roll(a, ATTN_WIDTH - HEAD_DIM // 2, 1),
                            pltpu.roll(a, HEAD_DIM // 2, 1))
        return a * cos + partner * sin

    q_ref[...] = rot(proj[:, 0:ATTN_WIDTH]) * Q_SCALE
    k = rot(proj[:, ATTN_WIDTH:2 * ATTN_WIDTH])
    v = proj[:, 2 * ATTN_WIDTH:3 * ATTN_WIDTH]
    k_ref[...] = k
    if transposed:
        kt_ref[...] = k.T
        vt_ref[...] = v.T
    else:
        v_ref[...] = v
    p_ref[...] = proj[:, 3 * ATTN_WIDTH:]


def _inproj_even(x, g, w, cos, sin, name, kv_t_shape=None, kv_t=None, layer=0):
    n = x.shape[0]
    tm = min(ROW_TILE, n)
    n_pos_blocks = cos.shape[0] // tm
    row = lambda i: (i, 0)
    out = jax.ShapeDtypeStruct((n, ATTN_WIDTH), F32)
    row_spec = pl.BlockSpec((tm, ATTN_WIDTH), row)
    in_specs = [
        pl.BlockSpec((tm, D_MODEL), row),
        _param_spec(g),
        _param_spec(w),
        pl.BlockSpec((tm, 128), lambda i: (i % n_pos_blocks, 0)),
        pl.BlockSpec((tm, 128), lambda i: (i % n_pos_blocks, 0)),
    ]
    args = [x, _param_arg(g), _param_arg(w), cos, sin]
    aliases = {}
    if kv_t_shape is None:
        out_specs, out_shape = [row_spec] * 4, [out] * 4
    else:
        seq_tiles = kv_t_shape[3] // tm
        t_spec = pl.BlockSpec((None, None, ATTN_WIDTH, tm),
                              lambda i: (layer, i // seq_tiles, 0, i % seq_tiles))
        t_shape = jax.ShapeDtypeStruct(kv_t_shape, F32)
        if kv_t is not None:
            aliases = {len(args): 2, len(args) + 1: 3}
            in_specs += [pl.BlockSpec(memory_space=pl.ANY)] * 2
            args += list(kv_t)
        out_specs = [row_spec, row_spec, t_spec, t_spec, row_spec]
        out_shape = [out, out, t_shape, t_shape, out]
    return pl.pallas_call(
        functools.partial(_inproj_even_kernel, transposed=kv_t_shape is not None),
        grid=(n // tm,),
        in_specs=in_specs,
        out_specs=out_specs,
        out_shape=out_shape,
        input_output_aliases=aliases,
        compiler_params=pltpu.CompilerParams(
            dimension_semantics=("arbitrary",), vmem_limit_bytes=VMEM_LIMIT),
        name=name,
    )(*args)


def _top3_mask(gate, cand, index_f, axis=-1):
    g = jnp.where(cand, gate, -jnp.inf)
    sel = jnp.zeros(gate.shape, jnp.bool_)
    for _ in range(MOBA_TOPK):
        m = jnp.max(g, axis=axis, keepdims=True)
        idx = jnp.min(jnp.where(g == m, index_f, 1e9), axis=axis, keepdims=True)
        pick = index_f == idx
        sel = jnp.logical_or(sel, pick)
        g = jnp.where(pick, -jnp.inf, g)
    return sel


def _attn_prompt_kernel(q_ref, k_ref, vt_ref, o_ref, kh_ref, vtb_ref):
    seq = k_ref.shape[0]
    n_blocks = seq // MOBA_BLOCK
    lane1 = lax.broadcasted_iota(jnp.int32, (1, 128), 1)
    lane_s = lax.broadcasted_iota(jnp.int32, (seq, 128), 1)
    k2 = k_ref[...]
    kmean = jnp.sum(k2.reshape(n_blocks, MOBA_BLOCK, 128), axis=1) * (1.0 / MOBA_BLOCK)
    gate_w = []
    for h in range(2):
        vtb_ref[h, 0:HEAD_DIM, :] = vt_ref[h * HEAD_DIM:(h + 1) * HEAD_DIM, :].astype(BF16)
        vtb_ref[h, HEAD_DIM:, :] = jnp.ones((V_ROWS - HEAD_DIM, seq), BF16)
        own = jnp.logical_and(lane_s >= h * HEAD_DIM, lane_s < (h + 1) * HEAD_DIM)
        kh_ref[h] = jnp.where(own, k2, 0.0).astype(BF16)
        own1 = jnp.logical_and(lane1 >= h * HEAD_DIM, lane1 < (h + 1) * HEAD_DIM)
        gate_w.append(jnp.where(own1, kmean, 0.0))

    blk_i = lax.broadcasted_iota(jnp.int32, (n_blocks, MOBA_BLOCK), 0)
    blk_f = blk_i.astype(F32)
    key_i = lax.broadcasted_iota(jnp.int32, (MOBA_BLOCK, MOBA_BLOCK), 0)
    qry_i = lax.broadcasted_iota(jnp.int32, (MOBA_BLOCK, MOBA_BLOCK), 1)
    causal = key_i <= qry_i

    def blk(j):
        return slice(j * MOBA_BLOCK, (j + 1) * MOBA_BLOCK)

    def scores_pass(n, h, st):
        q2 = q_ref[blk(n), :]
        q2b = q2.astype(BF16)
        bias = None
        if n > MOBA_TOPK:
            gate = lax.dot_general(gate_w[h], q2, _NT, precision=lax.Precision.HIGHEST,
                                   preferred_element_type=F32)
            cand = blk_i < n
            sel = _top3_mask(gate, cand, blk_f, axis=0)
            bias = jnp.where(jnp.logical_and(cand, jnp.logical_not(sel)), MASK_BIAS, 0.0)
        m = None
        s_blocks = []
        s_t = lax.dot_general(kh_ref[h, 0:(n + 1) * MOBA_BLOCK, :], q2b, _NT,
                              preferred_element_type=F32)
        for j in range(n + 1):
            s = s_t[blk(j)]
            if j == n:
                s = jnp.where(causal, s, -jnp.inf)
            mj = jnp.max(s.reshape(MOBA_BLOCK // 8, 8, MOBA_BLOCK), axis=0)
            if bias is not None and j < n:
                mj = mj + bias[j:j + 1]
            m = mj if m is None else jnp.maximum(m, mj)
            s_blocks.append(s)
            yield
        m = jnp.max(m, axis=0, keepdims=True)
        st["s"] = s_blocks
        st["shift"] = [m if (bias is None or j == n) else m - bias[j:j + 1]
                       for j in range(n + 1)]

    def output_pass(n, h, st):
        o_t = None
        for j in range(n + 1):
            p = jnp.exp2(st["s"][j] - st["shift"][j]).astype(BF16)
            oj = jnp.dot(vtb_ref[h, :, blk(j)], p, preferred_element_type=F32)
            o_t = oj if o_t is None else o_t + oj
            yield
        o_ref[h * HEAD_DIM:(h + 1) * HEAD_DIM, blk(n)] = (
            o_t[0:HEAD_DIM] / o_t[HEAD_DIM:HEAD_DIM + 1])

    units = [(n, h) for n in range(n_blocks) for h in range(2)]
    states = [dict() for _ in units]

    def all_scores():
        for u, unit in enumerate(units):
            yield from scores_pass(*unit, states[u])
            states[u]["ready"] = True
            yield

    def all_outputs():
        for u, unit in enumerate(units):
            ahead = min(u + ATTN_LEAD_UNITS - 1, len(units) - 1)
            while "ready" not in states[ahead]:
                yield
            yield from output_pass(*unit, states[u])

    _round_robin([all_scores(), all_outputs()])


def _attn_prompt(q, k, vt_all, layer, seq, name):
    n = q.shape[0]
    batch = n // seq
    n_pairs = ATTN_WIDTH // 128
    row_spec = pl.BlockSpec((seq, 128), lambda b, hp: (b, hp))
    return pl.pallas_call(
        _attn_prompt_kernel,
        grid=(batch, n_pairs),
        in_specs=[row_spec, row_spec,
                  pl.BlockSpec((None, None, 128, seq), lambda b, hp: (layer, b, hp, 0))],
        out_specs=pl.BlockSpec((None, 128, seq), lambda b, hp: (b, hp, 0)),
        out_shape=jax.ShapeDtypeStruct((batch, ATTN_WIDTH, seq), F32),
        scratch_shapes=[pltpu.VMEM((2, seq, 128), BF16), pltpu.VMEM((2, V_ROWS, seq), BF16)],
        compiler_params=pltpu.CompilerParams(
            dimension_semantics=("arbitrary", "arbitrary"), vmem_limit_bytes=VMEM_LIMIT),
        name=name,
    )(q, k, vt_all)


def _sample_attn_steps(q_ref, kn_ref, vn_ref, o_ref, row0, n_new, k_pages, v_pages):
    n_pages = len(k_pages)
    page = k_pages[0].shape[1]
    pages_per_block = MOBA_BLOCK // page
    n_blocks = n_pages // pages_per_block
    n_rows = n_new * N_HEADS

    lane = lax.broadcasted_iota(jnp.int32, (N_HEADS, ATTN_WIDTH), 1)
    head = lax.broadcasted_iota(jnp.int32, (N_HEADS, ATTN_WIDTH), 0)
    head_lanes = (lane // HEAD_DIM == head).astype(F32)
    row_t = lax.broadcasted_iota(jnp.int32, (n_rows, 1), 0) // N_HEADS
    lane_w = lax.broadcasted_iota(jnp.int32, (ATTN_WIDTH, 128), 1)
    lane_g = lax.broadcasted_iota(jnp.int32, (n_rows, 128), 1)
    lane_gf = lane_g.astype(F32)

    def block_of(pages, j):
        return jnp.concatenate([pages[pages_per_block * j + i][...]
                                for i in range(pages_per_block)], axis=1)

    q_rows = [q_ref[row0 + t:row0 + t + 1, :] * head_lanes for t in range(n_new)]
    qbd = jnp.concatenate(q_rows, axis=0)
    qbd_b = qbd.astype(BF16)
    gate_w = jnp.zeros((ATTN_WIDTH, 128), F32)
    scores = []
    for j in range(n_blocks):
        kj = block_of(k_pages, j)
        mean_j = jnp.sum(kj, axis=1, keepdims=True) * (1.0 / MOBA_BLOCK)
        gate_w = jnp.where(lane_w == j, mean_j, gate_w)
        scores.append(jnp.dot(qbd_b, kj.astype(BF16), preferred_element_type=F32))
        yield

    gate = jnp.dot(qbd, gate_w, precision=lax.Precision.HIGHEST,
                   preferred_element_type=F32)
    sel = _top3_mask(gate, lane_g < n_blocks, lane_gf).astype(F32)
    yield

    m = jnp.full((n_rows, 1), -jnp.inf, F32)
    for j in range(n_blocks):
        sel_j = jnp.sum(jnp.where(lane_g == j, sel, 0.0), axis=-1, keepdims=True) > 0.5
        scores[j] = jnp.where(sel_j, scores[j], -jnp.inf)
        m = jnp.maximum(m, jnp.max(scores[j], axis=-1, keepdims=True))
    s_new = []
    for t in range(n_new):
        sc = jnp.sum(qbd * kn_ref[row0 + t:row0 + t + 1, :], axis=-1, keepdims=True)
        sc = jnp.where(row_t >= t, sc, -jnp.inf)
        s_new.append(sc)
        m = jnp.maximum(m, sc)
    yield

    l = jnp.zeros((n_rows, 1), F32)
    acc = jnp.zeros((n_rows, ATTN_WIDTH), F32)
    for j in range(n_blocks):
        p = jnp.exp2(scores[j] - m)
        l = l + jnp.sum(p, axis=-1, keepdims=True)
        acc = acc + lax.dot_general(p.astype(BF16), block_of(v_pages, j).astype(BF16), _NT,
                                    preferred_element_type=F32)
        yield
    for t in range(n_new):
        p = jnp.exp2(s_new[t] - m)
        l = l + p
        acc = acc + p * vn_ref[row0 + t:row0 + t + 1, :]

    acc = acc / l * jnp.concatenate([head_lanes] * n_new, axis=0)
    o_ref[row0:row0 + n_new, :] = jnp.sum(acc.reshape(n_new, N_HEADS, ATTN_WIDTH), axis=1)


def _attn_sample_kernel(pt_ref, q_ref, kn_ref, vn_ref, *refs, n_pages, group):
    del pt_ref
    o_ref = refs[2 * group * n_pages]
    n_new = q_ref.shape[0] // group
    _round_robin(
        _sample_attn_steps(q_ref, kn_ref, vn_ref, o_ref, s * n_new, n_new,
                           refs[s * n_pages:(s + 1) * n_pages],
                           refs[(group + s) * n_pages:(group + s + 1) * n_pages])
        for s in range(group))


def _attn_sample(q, k_new, v_new, n_new, cache_k, cache_v, page_table, layer, name):
    n_seq = q.shape[0] // n_new
    n_pages = page_table.shape[1]
    page = cache_k.shape[3]
    group = SAMPLE_GROUP
    new_spec = pl.BlockSpec((group * n_new, ATTN_WIDTH), lambda a, pt: (a, 0))

    def page_spec(s, j):
        return pl.BlockSpec((None, None, ATTN_WIDTH, page),
                            lambda a, pt: (layer, pt[a * group + s, j], 0, 0))

    page_specs = [page_spec(s, j) for s in range(group) for j in range(n_pages)]
    grid_spec = pltpu.PrefetchScalarGridSpec(
        num_scalar_prefetch=1,
        grid=(n_seq // group,),
        in_specs=[new_spec] * 3 + page_specs * 2,
        out_specs=new_spec,
    )
    n_page_args = group * n_pages
    return pl.pallas_call(
        functools.partial(_attn_sample_kernel, n_pages=n_pages, group=group),
        grid_spec=grid_spec,
        out_shape=jax.ShapeDtypeStruct((n_seq * n_new, ATTN_WIDTH), F32),
        compiler_params=pltpu.CompilerParams(
            dimension_semantics=("arbitrary",), vmem_limit_bytes=VMEM_LIMIT),
        name=name,
    )(page_table, q, k_new, v_new, *([cache_k] * n_page_args), *([cache_v] * n_page_args))


def _pool_windows(read, tm, count):
    groups = []
    for g, w in enumerate(POOL_WINDOWS):
        cur = read(0, g)
        total = cur
        for back in range(1, w):
            total = total + read(back, g)
        groups.append(total / count(w) - cur)
    return groups


def _mix_residual(groups, attn, y, wpool_ref, pscale_ref, wo_ref):
    pooled = [jnp.dot(groups[g].astype(BF16), wpool_ref[g], preferred_element_type=F32)
              for g in range(len(POOL_WINDOWS))]
    pooled = jnp.concatenate(pooled, axis=1) * pscale_ref[...]
    mix = jnp.concatenate([attn, pooled], axis=1).astype(BF16)
    return y + jnp.dot(mix, wo_ref[...], preferred_element_type=F32)


def _mix_sample_kernel(*refs):
    n_groups = len(POOL_WINDOWS)
    attn_ref = refs[0]
    p_refs = refs[1:1 + n_groups]
    (hist_ref, y_ref, wpool_ref, pscale_ref, wo_ref, gffn_ref, wup_ref, wdn_ref,
     o_ref, pooled_ref) = refs[1 + n_groups:]
    tm = y_ref.shape[0]
    n_seq = hist_ref.shape[1]
    n_new = tm // n_seq
    for t in range(n_new):
        def read(back, g, t=t):
            if back <= t:
                return p_refs[g][pl.ds(t - back, n_seq, stride=n_new), :]
            return hist_ref[POOL_HIST + t - back, :, g * POOL_GROUP:(g + 1) * POOL_GROUP]
        groups = _pool_windows(read, n_seq, lambda w: float(w))
        for g in range(n_groups):
            pooled_ref[g, pl.ds(t, n_seq, stride=n_new), :] = groups[g]
    groups = [pooled_ref[g] for g in range(n_groups)]
    y = _mix_residual(groups, attn_ref[...], y_ref[...], wpool_ref, pscale_ref, wo_ref)
    o_ref[...] = _ffn(y, gffn_ref, wup_ref, wdn_ref)


def _mix_prompt_kernel(attn_ref, p_ref, halo_ref, y_ref, wpool_ref, pscale_ref, wo_ref,
                       gffn_ref, wup_ref, wdn_ref, o_ref, xp_ref, *, seq_tiles):
    tm = p_ref.shape[0]
    seq_pos0 = (pl.program_id(0) % seq_tiles) * tm
    xp_ref[0:HALO, :] = jnp.where(seq_pos0 == 0, 0.0, halo_ref[...])
    xp_ref[HALO:, :] = p_ref[...]
    pos1 = (seq_pos0 + lax.broadcasted_iota(jnp.int32, (tm, 1), 0) + 1).astype(F32)

    def read(back, g):
        return xp_ref[pl.ds(HALO - back, tm), g * POOL_GROUP:(g + 1) * POOL_GROUP]
    groups = _pool_windows(read, tm, lambda w: jnp.minimum(pos1, float(w)))
    y = _mix_residual(groups, attn_ref[...].T, y_ref[...], wpool_ref, pscale_ref, wo_ref)
    o_ref[...] = _ffn(y, gffn_ref, wup_ref, wdn_ref)


def _mix_sample(attn, p, hist, y, wpool, pscale, wo, gffn, wup, wdn, name):
    tm = y.shape[0]
    row = lambda i: (0, 0)
    n_groups = len(POOL_WINDOWS)
    params = [wpool, pscale, wo, gffn, wup, wdn]
    group_view = lambda g: pl.BlockSpec((tm, POOL_GROUP), lambda i: (0, g))
    return pl.pallas_call(
        _mix_sample_kernel,
        grid=(1,),
        in_specs=([pl.BlockSpec((tm, ATTN_WIDTH), row)]
                  + [group_view(g) for g in range(n_groups)]
                  + [_const_spec(hist.shape), pl.BlockSpec((tm, D_MODEL), row)]
                  + [_param_spec(a) for a in params]),
        out_specs=pl.BlockSpec((tm, D_MODEL), row),
        out_shape=jax.ShapeDtypeStruct((tm, D_MODEL), F32),
        scratch_shapes=[pltpu.VMEM((n_groups, tm, POOL_GROUP), F32)],
        compiler_params=pltpu.CompilerParams(
            dimension_semantics=("arbitrary",), vmem_limit_bytes=VMEM_LIMIT),
        name=name,
    )(attn, *([p] * n_groups), hist, y, *[_param_arg(a) for a in params])


def _mix_prompt(attn_t, p, y, wpool, pscale, wo, gffn, wup, wdn, seq, name):
    n = y.shape[0]
    tm = MIX_TILE
    assert seq % tm == 0
    seq_tiles = seq // tm
    blocks_per_tile = tm // HALO
    params = [wpool, pscale, wo, gffn, wup, wdn]
    row = lambda i: (i, 0)
    return pl.pallas_call(
        functools.partial(_mix_prompt_kernel, seq_tiles=seq_tiles),
        grid=(n // tm,),
        in_specs=[
            pl.BlockSpec((None, ATTN_WIDTH, tm), lambda i: (i // seq_tiles, 0, i % seq_tiles)),
            pl.BlockSpec((tm, POOL_WIDTH), row),
            pl.BlockSpec((HALO, POOL_WIDTH),
                         lambda i: (jnp.maximum(i * blocks_per_tile - 1, 0), 0)),
            pl.BlockSpec((tm, D_MODEL), row),
        ] + [_param_spec(a) for a in params],
        out_specs=pl.BlockSpec((tm, D_MODEL), row),
        out_shape=jax.ShapeDtypeStruct((n, D_MODEL), F32),
        scratch_shapes=[pltpu.VMEM((tm + HALO, POOL_WIDTH), F32)],
        compiler_params=pltpu.CompilerParams(
            dimension_semantics=("arbitrary",), vmem_limit_bytes=VMEM_LIMIT),
        name=name,
    )(attn_t, p, p, y, *[_param_arg(a) for a in params])


def _odd_kernel(*refs, emit_v, final_norm):
    (x_ref, gmix_ref, win_ref, lng_ref, lnb_ref, ws_ref, bs_ref, wo_ref,
     gffn_ref, wup_ref, wdn_ref) = refs[:11]
    rest = list(refs[11:])
    gfin_ref = rest.pop(0) if final_norm else None
    o_ref = rest.pop(0)
    v_ref = rest.pop(0) if emit_v else None
    tm = x_ref.shape[0]

    x = x_ref[...]
    xn = _rms(x, gmix_ref[...]).astype(BF16)
    z = jax.nn.gelu(jnp.dot(xn, win_ref[...], preferred_element_type=F32))
    u = z[:, :D_MODEL]
    v = z[:, D_MODEL:]
    mu = jnp.mean(v, axis=-1, keepdims=True)
    vc = v - mu
    var = jnp.mean(vc * vc, axis=-1, keepdims=True)
    v = vc * lax.rsqrt(var + LN_EPS) * lng_ref[...] + lnb_ref[...]
    if emit_v:
        v_ref[...] = v
    vb = v.astype(BF16)
    bias = bs_ref[...]
    chunks = []
    for c in range(tm // SGU_CHUNK):
        rows = slice(c * SGU_CHUNK, (c + 1) * SGU_CHUNK)
        cols = [jnp.dot(ws_ref[g], vb[rows, g * 128:(g + 1) * 128],
                        preferred_element_type=F32) for g in range(N_SGU_GROUPS)]
        chunks.append(jnp.concatenate(cols, axis=1) + bias)
    gated = (u * jnp.concatenate(chunks, axis=0)).astype(BF16)
    y = x + jnp.dot(gated, wo_ref[...], preferred_element_type=F32)
    y = _ffn(y, gffn_ref, wup_ref, wdn_ref)
    if final_norm:
        y = _rms(y, gfin_ref[...])
    o_ref[...] = y


def _odd_layer(x, gmix, win, lng, lnb, ws, bs, wo, gffn, wup, wdn, gfin, emit_v, name):
    n = x.shape[0]
    tm = min(ROW_TILE, n)
    row = lambda i: (i, 0)
    final_norm = gfin is not None
    params = [gmix, win, lng, lnb, ws, bs, wo, gffn, wup, wdn]
    if final_norm:
        params.append(gfin)
    args = [x] + [_param_arg(a) for a in params]
    in_specs = [pl.BlockSpec((tm, D_MODEL), row)] + [_param_spec(a) for a in params]
    out_specs = [pl.BlockSpec((tm, D_MODEL), row)]
    out_shape = [jax.ShapeDtypeStruct((n, D_MODEL), F32)]
    if emit_v:
        out_specs.append(pl.BlockSpec((tm, D_MODEL), row))
        out_shape.append(jax.ShapeDtypeStruct((n, D_MODEL), F32))
    outs = pl.pallas_call(
        functools.partial(_odd_kernel, emit_v=emit_v, final_norm=final_norm),
        grid=(n // tm,),
        in_specs=in_specs,
        out_specs=out_specs,
        out_shape=out_shape,
        compiler_params=pltpu.CompilerParams(
            dimension_semantics=("arbitrary",), vmem_limit_bytes=VMEM_LIMIT),
        name=name,
    )(*args)
    return outs if emit_v else (outs[0], None)


def _rope_tables(pos):
    inv = ROPE_THETA ** (-jnp.arange(0, HEAD_DIM, 2, dtype=F32) / HEAD_DIM)
    ang = pos.astype(F32)[:, None] * inv[None, :]
    cos, sin = jnp.cos(ang), jnp.sin(ang)
    return (jnp.concatenate([cos, cos, cos, cos], axis=1),
            jnp.concatenate([-sin, sin, -sin, sin], axis=1))


def _sgu_weights(w_s, b_s, length):
    tril = jnp.tril(jnp.ones((length, length), dtype=bool))
    w = jnp.where(tril, w_s[:, :length, :length], 0.0)
    reps = SGU_CHUNK // length
    if reps > 1:
        eye = jnp.eye(reps, dtype=w.dtype)
        w = jnp.einsum('ab,gts->gatbs', eye, w).reshape(
            N_SGU_GROUPS, SGU_CHUNK, SGU_CHUNK)
    b = jnp.tile(b_s[:, :length], (1, reps))
    b = jnp.repeat(b.T, D_MODEL // N_SGU_GROUPS, axis=1)
    return w.astype(BF16), b


def kernel(x_prompt, x_sample, cache_k, cache_v, state_pool, page_table, norm_mix, w_in_even, w_o_even, w_pool, pool_scale, w_in_odd, sgu_ln_g, sgu_ln_b, sgu_w, sgu_b, w_o_odd, norm_ffn, w_ffn_up, w_ffn_down, norm_final):
    batch, seq, _ = x_prompt.shape
    n_seq, n_new, _ = x_sample.shape
    depth = norm_mix.shape[0]
    n_phys, page = cache_k.shape[1], cache_k.shape[2]
    past_len = page_table.shape[1] * page

    yp = x_prompt.reshape(batch * seq, D_MODEL)
    ys = x_sample.reshape(n_seq * n_new, D_MODEL)
    ck = jnp.transpose(cache_k, (0, 1, 3, 4, 2)).reshape(cache_k.shape[0], n_phys, ATTN_WIDTH, page)
    cv = jnp.transpose(cache_v, (0, 1, 3, 4, 2)).reshape(cache_v.shape[0], n_phys, ATTN_WIDTH, page)

    cos_p, sin_p = _rope_tables(jnp.arange(seq))
    cos_s, sin_s = _rope_tables(past_len + jnp.arange(n_seq * n_new) % n_new)

    w_in_even_b = w_in_even.astype(BF16)
    w_o_even_b = w_o_even.astype(BF16)
    w_pool_b = w_pool.astype(BF16)
    w_in_odd_b = w_in_odd.astype(BF16)
    w_o_odd_b = w_o_odd.astype(BF16)
    w_up_b = w_ffn_up.astype(BF16)
    w_dn_b = w_ffn_down.astype(BF16)

    sample_k, sample_v = [], []
    prompt_pool, sample_pool, sample_sgu = [], [], []
    n_even = (depth + 1) // 2
    kv_t = None
    for l in range(depth):
        gmix = (norm_mix[:, None, :], l)
        gffn = (norm_ffn[:, None, :], l)
        w_up_l, w_dn_l = (w_up_b, l), (w_dn_b, l)
        if l % 2 == 0:
            i = l // 2
            pscale = (pool_scale[:, None, :], i)
            w_in_l, w_o_l, w_pool_l = (w_in_even_b, i), (w_o_even_b, i), (w_pool_b, i)
            q, k, kt_all, vt_all, p = _inproj_even(
                yp, gmix, w_in_l, cos_p, sin_p, f"inproj_p{l}",
                kv_t_shape=(n_even, batch, ATTN_WIDTH, seq), kv_t=kv_t, layer=i)
            kv_t = (kt_all, vt_all)
            attn = _attn_prompt(q, k, vt_all, i, seq, f"attn_p{l}")
            yp = _mix_prompt(attn, p, yp, w_pool_l, pscale, w_o_l, gffn, w_up_l, w_dn_l,
                             seq, f"mix_p{l}")
            prompt_pool.append(p.reshape(batch, seq, POOL_WIDTH)[:, seq - POOL_HIST:, :])

            q, k, v, p = _inproj_even(ys, gmix, w_in_l, cos_s, sin_s, f"inproj_s{l}")
            attn = _attn_sample(q, k, v, n_new, ck, cv, page_table, i, f"attn_s{l}")
            hist = jnp.transpose(state_pool[i], (1, 0, 2))
            ys = _mix_sample(attn, p, hist, ys, w_pool_l,
                             pscale, w_o_l, gffn, w_up_l, w_dn_l, f"mix_s{l}")
            sample_k.append(k.reshape(n_seq, n_new, N_HEADS, HEAD_DIM))
            sample_v.append(v.reshape(n_seq, n_new, N_HEADS, HEAD_DIM))
            sample_pool.append(jnp.concatenate(
                [state_pool[i], p.reshape(n_seq, n_new, POOL_WIDTH)], axis=1)[:, -POOL_HIST:, :])
        else:
            j = l // 2
            gfin = norm_final[None, :] if l == depth - 1 else None
            lng = (sgu_ln_g[:, None, :], j)
            lnb = (sgu_ln_b[:, None, :], j)
            w_in_l, w_o_l = (w_in_odd_b, j), (w_o_odd_b, j)
            ws_p, bs_p = _sgu_weights(sgu_w[j], sgu_b[j], SGU_CHUNK)
            ws_s, bs_s = _sgu_weights(sgu_w[j], sgu_b[j], n_new)
            yp, _ = _odd_layer(yp, gmix, w_in_l, lng, lnb, ws_p, bs_p, w_o_l,
                               gffn, w_up_l, w_dn_l, gfin, False, f"odd_p{l}")
            ys, v_rows = _odd_layer(ys, gmix, w_in_l, lng, lnb, ws_s, bs_s, w_o_l,
                                    gffn, w_up_l, w_dn_l, gfin, True, f"odd_s{l}")
            sample_sgu.append(v_rows.reshape(n_seq, n_new, D_MODEL))
    if depth % 2 == 1:
        raise NotImplementedError("the final norm is fused into an odd last layer")
    prompt_k, prompt_v = [
        jnp.transpose(t.reshape(n_even, batch, N_HEADS, HEAD_DIM, seq), (0, 1, 4, 2, 3))
        for t in kv_t]
    return (yp.reshape(batch, seq, D_MODEL), ys.reshape(n_seq, n_new, D_MODEL),
            prompt_k, prompt_v,
            jnp.stack(sample_k), jnp.stack(sample_v),
            jnp.stack(prompt_pool), jnp.stack(sample_pool),
            jnp.stack(sample_sgu))
```

```python
import functools

import jax
import jax.numpy as jnp
from jax import lax
from jax.experimental import pallas as pl
from jax.experimental.pallas import tpu as pltpu

F32 = jnp.float32
BF16 = jnp.bfloat16

D_MODEL = 1024
HEAD_DIM = 64
N_HEADS = 8
ATTN_WIDTH = 512
POOL_WIDTH = 512
POOL_WINDOWS = (2, 4, 8, 16)
POOL_GROUP = 128
POOL_HIST = 15
MOBA_BLOCK = 256
MOBA_TOPK = 3
ROPE_THETA = 10000.0
SGU_CHUNK = 128
N_SGU_GROUPS = 8
D_FF = 4096
NORM_EPS = 1e-6
LN_EPS = 1e-5

ROW_TILE = 1024
MIX_TILE = 512
FF_CHUNK = 1024
HALO = 16
MASK_BIAS = -1e30
Q_SCALE = HEAD_DIM ** -0.5 * 1.4426950408889634
V_ROWS = HEAD_DIM + 16
SAMPLE_GROUP = 2
ATTN_LEAD_UNITS = 2
VMEM_LIMIT = 56 * 1024 * 1024

_NT = (((1,), (1,)), ((), ()))


def _rms(x, g):
    ms = jnp.mean(x * x, axis=-1, keepdims=True)
    return x * lax.rsqrt(ms + NORM_EPS) * g


def _ffn(y, g_ref, wup_ref, wdn_ref):
    hn = _rms(y, g_ref[...]).astype(BF16)
    acc = y
    for c in range(D_FF // FF_CHUNK):
        sl = slice(c * FF_CHUNK, (c + 1) * FF_CHUNK)
        h = jnp.dot(hn, wup_ref[:, sl], preferred_element_type=F32)
        h = jnp.maximum(h, 0.0)
        h = (h * h).astype(BF16)
        acc = acc + jnp.dot(h, wdn_ref[sl, :], preferred_element_type=F32)
    return acc


def _round_robin(gens):
    gens = list(gens)
    while gens:
        for g in list(gens):
            if next(g, StopIteration) is StopIteration:
                gens.remove(g)


def _const_spec(shape):
    nd = len(shape)
    return pl.BlockSpec(shape, lambda *_: (0,) * nd, pipeline_mode=pl.Buffered(1))


def _param_spec(a):
    if isinstance(a, tuple):
        w, layer = a
        nd = w.ndim - 1
        return pl.BlockSpec((None,) + w.shape[1:], lambda *_: (layer,) + (0,) * nd,
                            pipeline_mode=pl.Buffered(1))
    return _const_spec(a.shape)


def _param_arg(a):
    return a[0] if isinstance(a, tuple) else a


def _cast_jobs(jobs, n_steps, flat_step):
    in_specs, args, out_specs, out_shapes = [], [], [], []
    for w, layer in jobs:
        rows, cols = w.shape[1:]
        chunk = rows // n_steps
        assert chunk * n_steps == rows and chunk % 16 == 0
        in_specs.append(pl.BlockSpec((None, chunk, cols),
                                     lambda *idx, layer=layer: (layer, flat_step(*idx), 0)))
        out_specs.append(pl.BlockSpec((chunk, cols), lambda *idx: (flat_step(*idx), 0)))
        args.append(w)
        out_shapes.append(jax.ShapeDtypeStruct((rows, cols), BF16))
    return in_specs, args, out_specs, out_shapes


def _with_cast_jobs(kernel_fn, n_in, n_out, n_jobs):
    if not n_jobs:
        return kernel_fn

    def wrapped(*refs):
        a, b, c = n_in + n_jobs, n_in + n_jobs + n_out, n_in + 2 * n_jobs + n_out
        for src, dst in zip(refs[n_in:a], refs[b:c]):
            dst[...] = src[...].astype(BF16)
        kernel_fn(*refs[:n_in], *refs[a:b], *refs[c:])
    return wrapped


def _inproj_even_kernel(x_ref, g_ref, w_ref, cos_ref, sin_ref, *refs, transposed):
    if transposed:
        q_ref, k_ref, kt_ref, vt_ref, p_ref = refs[-5:]
    else:
        q_ref, k_ref, v_ref, p_ref = refs
    xn = _rms(x_ref[...], g_ref[...]).astype(BF16)
    proj = jnp.dot(xn, w_ref[...], preferred_element_type=F32)
    cos = jnp.concatenate([cos_ref[...]] * 4, axis=1)
    sin = jnp.concatenate([sin_ref[...]] * 4, axis=1)
    lane = lax.broadcasted_iota(jnp.int32, cos.shape, 1) % HEAD_DIM
    first_half = lane < HEAD_DIM // 2

    def rot(a):
        partner = jnp.where(first_half,
                            pltpu.roll(a, ATTN_WIDTH - HEAD_DIM // 2, 1),
                            pltpu.roll(a, HEAD_DIM // 2, 1))
        return a * cos + partner * sin

    q_ref[...] = rot(proj[:, 0:ATTN_WIDTH]) * Q_SCALE
    k = rot(proj[:, ATTN_WIDTH:2 * ATTN_WIDTH])
    v = proj[:, 2 * ATTN_WIDTH:3 * ATTN_WIDTH]
    k_ref[...] = k
    if transposed:
        kt_ref[...] = k.T
        vt_ref[...] = v.T
    else:
        v_ref[...] = v
    p_ref[...] = proj[:, 3 * ATTN_WIDTH:]


def _inproj_even(x, g, w, cos, sin, name, kv_t_shape=None, kv_t=None, layer=0):
    n = x.shape[0]
    tm = min(ROW_TILE, n)
    n_pos_blocks = cos.shape[0] // tm
    row = lambda i: (i, 0)
    out = jax.ShapeDtypeStruct((n, ATTN_WIDTH), F32)
    row_spec = pl.BlockSpec((tm, ATTN_WIDTH), row)
    in_specs = [
        pl.BlockSpec((tm, D_MODEL), row),
        _param_spec(g),
        _param_spec(w),
        pl.BlockSpec((tm, 128), lambda i: (i % n_pos_blocks, 0)),
        pl.BlockSpec((tm, 128), lambda i: (i % n_pos_blocks, 0)),
    ]
    args = [x, _param_arg(g), _param_arg(w), cos, sin]
    aliases = {}
    if kv_t_shape is None:
        out_specs, out_shape = [row_spec] * 4, [out] * 4
    else:
        seq_tiles = kv_t_shape[3] // tm
        t_spec = pl.BlockSpec((None, None, ATTN_WIDTH, tm),
                              lambda i: (layer, i // seq_tiles, 0, i % seq_tiles))
        t_shape = jax.ShapeDtypeStruct(kv_t_shape, F32)
        if kv_t is not None:
            aliases = {len(args): 2, len(args) + 1: 3}
            in_specs += [pl.BlockSpec(memory_space=pl.ANY)] * 2
            args += list(kv_t)
        out_specs = [row_spec, row_spec, t_spec, t_spec, row_spec]
        out_shape = [out, out, t_shape, t_shape, out]
    return pl.pallas_call(
        functools.partial(_inproj_even_kernel, transposed=kv_t_shape is not None),
        grid=(n // tm,),
        in_specs=in_specs,
        out_specs=out_specs,
        out_shape=out_shape,
        input_output_aliases=aliases,
        compiler_params=pltpu.CompilerParams(
            dimension_semantics=("arbitrary",), vmem_limit_bytes=VMEM_LIMIT),
        name=name,
    )(*args)


def _top3_mask(gate, cand, index_f, axis=-1):
    g = jnp.where(cand, gate, -jnp.inf)
    sel = jnp.zeros(gate.shape, jnp.bool_)
    for _ in range(MOBA_TOPK):
        m = jnp.max(g, axis=axis, keepdims=True)
        idx = jnp.min(jnp.where(g == m, index_f, 1e9), axis=axis, keepdims=True)
        pick = index_f == idx
        sel = jnp.logical_or(sel, pick)
        g = jnp.where(pick, -jnp.inf, g)
    return sel


def _attn_prompt_kernel(q_ref, k_ref, vt_ref, o_ref, kh_ref, vtb_ref):
    seq = k_ref.shape[0]
    n_blocks = seq // MOBA_BLOCK
    lane1 = lax.broadcasted_iota(jnp.int32, (1, 128), 1)
    lane_s = lax.broadcasted_iota(jnp.int32, (seq, 128), 1)
    k2 = k_ref[...]
    kmean = jnp.sum(k2.reshape(n_blocks, MOBA_BLOCK, 128), axis=1) * (1.0 / MOBA_BLOCK)
    gate_w = []
    for h in range(2):
        vtb_ref[h, 0:HEAD_DIM, :] = vt_ref[h * HEAD_DIM:(h + 1) * HEAD_DIM, :].astype(BF16)
        vtb_ref[h, HEAD_DIM:, :] = jnp.ones((V_ROWS - HEAD_DIM, seq), BF16)
        own = jnp.logical_and(lane_s >= h * HEAD_DIM, lane_s < (h + 1) * HEAD_DIM)
        kh_ref[h] = jnp.where(own, k2, 0.0).astype(BF16)
        own1 = jnp.logical_and(lane1 >= h * HEAD_DIM, lane1 < (h + 1) * HEAD_DIM)
        gate_w.append(jnp.where(own1, kmean, 0.0))

    blk_i = lax.broadcasted_iota(jnp.int32, (n_blocks, MOBA_BLOCK), 0)
    blk_f = blk_i.astype(F32)
    key_i = lax.broadcasted_iota(jnp.int32, (MOBA_BLOCK, MOBA_BLOCK), 0)
    qry_i = lax.broadcasted_iota(jnp.int32, (MOBA_BLOCK, MOBA_BLOCK), 1)
    causal = key_i <= qry_i

    def blk(j):
        return slice(j * MOBA_BLOCK, (j + 1) * MOBA_BLOCK)

    def scores_pass(n, h, st):
        q2 = q_ref[blk(n), :]
        q2b = q2.astype(BF16)
        bias = None
        if n > MOBA_TOPK:
            gate = lax.dot_general(gate_w[h], q2, _NT, precision=lax.Precision.HIGHEST,
                                   preferred_element_type=F32)
            cand = blk_i < n
            sel = _top3_mask(gate, cand, blk_f, axis=0)
            bias = jnp.where(jnp.logical_and(cand, jnp.logical_not(sel)), MASK_BIAS, 0.0)
        m = None
        s_blocks = []
        s_t = lax.dot_general(kh_ref[h, 0:(n + 1) * MOBA_BLOCK, :], q2b, _NT,
                              preferred_element_type=F32)
        for j in range(n + 1):
            s = s_t[blk(j)]
            if j == n:
                s = jnp.where(causal, s, -jnp.inf)
            mj = jnp.max(s.reshape(MOBA_BLOCK // 8, 8, MOBA_BLOCK), axis=0)
            if bias is not None and j < n:
                mj = mj + bias[j:j + 1]
            m = mj if m is None else jnp.maximum(m, mj)
            s_blocks.append(s)
            yield
        m = jnp.max(m, axis=0, keepdims=True)
        st["s"] = s_blocks
        st["shift"] = [m if (bias is None or j == n) else m - bias[j:j + 1]
                       for j in range(n + 1)]

    def output_pass(n, h, st):
        o_t = None
        for j in range(n + 1):
            p = jnp.exp2(st["s"][j] - st["shift"][j]).astype(BF16)
            oj = jnp.dot(vtb_ref[h, :, blk(j)], p, preferred_element_type=F32)
            o_t = oj if o_t is None else o_t + oj
            yield
        o_ref[h * HEAD_DIM:(h + 1) * HEAD_DIM, blk(n)] = (
            o_t[0:HEAD_DIM] / o_t[HEAD_DIM:HEAD_DIM + 1])

    units = [(n, h) for n in range(n_blocks) for h in range(2)]
    states = [dict() for _ in units]

    def all_scores():
        for u, unit in enumerate(units):
            yield from scores_pass(*unit, states[u])
            states[u]["ready"] = True
            yield

    def all_outputs():
        for u, unit in enumerate(units):
            ahead = min(u + ATTN_LEAD_UNITS - 1, len(units) - 1)
            while "ready" not in states[ahead]:
                yield
            yield from output_pass(*unit, states[u])

    _round_robin([all_scores(), all_outputs()])


def _attn_prompt(q, k, vt_all, layer, seq, name, cast=()):
    n = q.shape[0]
    batch = n // seq
    n_pairs = ATTN_WIDTH // 128
    row_spec = pl.BlockSpec((seq, 128), lambda b, hp: (b, hp))
    c_in, c_args, c_out, c_shapes = _cast_jobs(cast, batch * n_pairs,
                                               lambda b, hp: b * n_pairs + hp)
    return pl.pallas_call(
        _with_cast_jobs(_attn_prompt_kernel, 3, 1, len(cast)),
        grid=(batch, n_pairs),
        in_specs=[row_spec, row_spec,
                  pl.BlockSpec((None, None, 128, seq), lambda b, hp: (layer, b, hp, 0))] + c_in,
        out_specs=[pl.BlockSpec((None, 128, seq), lambda b, hp: (b, hp, 0))] + c_out,
        out_shape=[jax.ShapeDtypeStruct((batch, ATTN_WIDTH, seq), F32)] + c_shapes,
        scratch_shapes=[pltpu.VMEM((2, seq, 128), BF16), pltpu.VMEM((2, V_ROWS, seq), BF16)],
        compiler_params=pltpu.CompilerParams(
            dimension_semantics=("arbitrary", "arbitrary"), vmem_limit_bytes=VMEM_LIMIT),
        name=name,
    )(q, k, vt_all, *c_args)


def _sample_attn_steps(q_ref, kn_ref, vn_ref, o_ref, row0, n_new, k_pages, v_pages):
    n_pages = len(k_pages)
    page = k_pages[0].shape[1]
    pages_per_block = MOBA_BLOCK // page
    n_blocks = n_pages // pages_per_block
    n_rows = n_new * N_HEADS

    lane = lax.broadcasted_iota(jnp.int32, (N_HEADS, ATTN_WIDTH), 1)
    head = lax.broadcasted_iota(jnp.int32, (N_HEADS, ATTN_WIDTH), 0)
    head_lanes = (lane // HEAD_DIM == head).astype(F32)
    row_t = lax.broadcasted_iota(jnp.int32, (n_rows, 1), 0) // N_HEADS
    lane_w = lax.broadcasted_iota(jnp.int32, (ATTN_WIDTH, 128), 1)
    lane_g = lax.broadcasted_iota(jnp.int32, (n_rows, 128), 1)
    lane_gf = lane_g.astype(F32)

    def block_of(pages, j):
        return jnp.concatenate([pages[pages_per_block * j + i][...]
                                for i in range(pages_per_block)], axis=1)

    q_rows = [q_ref[row0 + t:row0 + t + 1, :] * head_lanes for t in range(n_new)]
    qbd = jnp.concatenate(q_rows, axis=0)
    qbd_b = qbd.astype(BF16)
    gate_w = jnp.zeros((ATTN_WIDTH, 128), F32)
    scores = []
    for j in range(n_blocks):
        kj = block_of(k_pages, j)
        mean_j = jnp.sum(kj, axis=1, keepdims=True) * (1.0 / MOBA_BLOCK)
        gate_w = jnp.where(lane_w == j, mean_j, gate_w)
        scores.append(jnp.dot(qbd_b, kj.astype(BF16), preferred_element_type=F32))
        yield

    gate = jnp.dot(qbd, gate_w, precision=lax.Precision.HIGHEST,
                   preferred_element_type=F32)
    sel = _top3_mask(gate, lane_g < n_blocks, lane_gf).astype(F32)
    yield

    m = jnp.full((n_rows, 1), -jnp.inf, F32)
    for j in range(n_blocks):
        sel_j = jnp.sum(jnp.where(lane_g == j, sel, 0.0), axis=-1, keepdims=True) > 0.5
        scores[j] = jnp.where(sel_j, scores[j], -jnp.inf)
        m = jnp.maximum(m, jnp.max(scores[j], axis=-1, keepdims=True))
    s_new = []
    for t in range(n_new):
        sc = jnp.sum(qbd * kn_ref[row0 + t:row0 + t + 1, :], axis=-1, keepdims=True)
        sc = jnp.where(row_t >= t, sc, -jnp.inf)
        s_new.append(sc)
        m = jnp.maximum(m, sc)
    yield

    l = jnp.zeros((n_rows, 1), F32)
    acc = jnp.zeros((n_rows, ATTN_WIDTH), F32)
    for j in range(n_blocks):
        p = jnp.exp2(scores[j] - m)
        l = l + jnp.sum(p, axis=-1, keepdims=True)
        acc = acc + lax.dot_general(p.astype(BF16), block_of(v_pages, j).astype(BF16), _NT,
                                    preferred_element_type=F32)
        yield
    for t in range(n_new):
        p = jnp.exp2(s_new[t] - m)
        l = l + p
        acc = acc + p * vn_ref[row0 + t:row0 + t + 1, :]

    acc = acc / l * jnp.concatenate([head_lanes] * n_new, axis=0)
    o_ref[row0:row0 + n_new, :] = jnp.sum(acc.reshape(n_new, N_HEADS, ATTN_WIDTH), axis=1)


def _attn_sample_kernel(pt_ref, q_ref, kn_ref, vn_ref, *refs, n_pages, group):
    del pt_ref
    o_ref = refs[2 * group * n_pages]
    n_new = q_ref.shape[0] // group
    _round_robin(
        _sample_attn_steps(q_ref, kn_ref, vn_ref, o_ref, s * n_new, n_new,
                           refs[s * n_pages:(s + 1) * n_pages],
                           refs[(group + s) * n_pages:(group + s + 1) * n_pages])
        for s in range(group))


def _attn_sample(q, k_new, v_new, n_new, cache_k, cache_v, page_table, layer, name):
    n_seq = q.shape[0] // n_new
    n_pages = page_table.shape[1]
    page = cache_k.shape[3]
    group = SAMPLE_GROUP
    new_spec = pl.BlockSpec((group * n_new, ATTN_WIDTH), lambda a, pt: (a, 0))

    def page_spec(s, j):
        return pl.BlockSpec((None, None, ATTN_WIDTH, page),
                            lambda a, pt: (layer, pt[a * group + s, j], 0, 0))

    page_specs = [page_spec(s, j) for s in range(group) for j in range(n_pages)]
    grid_spec = pltpu.PrefetchScalarGridSpec(
        num_scalar_prefetch=1,
        grid=(n_seq // group,),
        in_specs=[new_spec] * 3 + page_specs * 2,
        out_specs=new_spec,
    )
    n_page_args = group * n_pages
    return pl.pallas_call(
        functools.partial(_attn_sample_kernel, n_pages=n_pages, group=group),
        grid_spec=grid_spec,
        out_shape=jax.ShapeDtypeStruct((n_seq * n_new, ATTN_WIDTH), F32),
        compiler_params=pltpu.CompilerParams(
            dimension_semantics=("arbitrary",), vmem_limit_bytes=VMEM_LIMIT),
        name=name,
    )(page_table, q, k_new, v_new, *([cache_k] * n_page_args), *([cache_v] * n_page_args))


def _pool_windows(read, tm, count):
    groups = []
    for g, w in enumerate(POOL_WINDOWS):
        cur = read(0, g)
        total = cur
        for back in range(1, w):
            total = total + read(back, g)
        groups.append(total / count(w) - cur)
    return groups


def _mix_residual(groups, attn, y, wpool_ref, pscale_ref, wo_ref):
    pooled = [jnp.dot(groups[g].astype(BF16), wpool_ref[g], preferred_element_type=F32)
              for g in range(len(POOL_WINDOWS))]
    pooled = jnp.concatenate(pooled, axis=1) * pscale_ref[...]
    mix = jnp.concatenate([attn, pooled], axis=1).astype(BF16)
    return y + jnp.dot(mix, wo_ref[...], preferred_element_type=F32)


def _mix_sample_kernel(*refs):
    n_groups = len(POOL_WINDOWS)
    attn_ref = refs[0]
    p_refs = refs[1:1 + n_groups]
    (hist_ref, y_ref, wpool_ref, pscale_ref, wo_ref, gffn_ref, wup_ref, wdn_ref,
     o_ref, pooled_ref) = refs[1 + n_groups:]
    tm = y_ref.shape[0]
    n_seq = hist_ref.shape[1]
    n_new = tm // n_seq
    for t in range(n_new):
        def read(back, g, t=t):
            if back <= t:
                return p_refs[g][pl.ds(t - back, n_seq, stride=n_new), :]
            return hist_ref[POOL_HIST + t - back, :, g * POOL_GROUP:(g + 1) * POOL_GROUP]
        groups = _pool_windows(read, n_seq, lambda w: float(w))
        for g in range(n_groups):
            pooled_ref[g, pl.ds(t, n_seq, stride=n_new), :] = groups[g]
    groups = [pooled_ref[g] for g in range(n_groups)]
    y = _mix_residual(groups, attn_ref[...], y_ref[...], wpool_ref, pscale_ref, wo_ref)
    o_ref[...] = _ffn(y, gffn_ref, wup_ref, wdn_ref)


def _mix_prompt_kernel(attn_ref, p_ref, halo_ref, y_ref, wpool_ref, pscale_ref, wo_ref,
                       gffn_ref, wup_ref, wdn_ref, o_ref, xp_ref, *, seq_tiles):
    tm = p_ref.shape[0]
    seq_pos0 = (pl.program_id(0) % seq_tiles) * tm
    xp_ref[0:HALO, :] = jnp.where(seq_pos0 == 0, 0.0, halo_ref[...])
    xp_ref[HALO:, :] = p_ref[...]
    pos1 = (seq_pos0 + lax.broadcasted_iota(jnp.int32, (tm, 1), 0) + 1).astype(F32)

    def read(back, g):
        return xp_ref[pl.ds(HALO - back, tm), g * POOL_GROUP:(g + 1) * POOL_GROUP]
    groups = _pool_windows(read, tm, lambda w: jnp.minimum(pos1, float(w)))
    y = _mix_residual(groups, attn_ref[...].T, y_ref[...], wpool_ref, pscale_ref, wo_ref)
    o_ref[...] = _ffn(y, gffn_ref, wup_ref, wdn_ref)


def _mix_sample(attn, p, hist, y, wpool, pscale, wo, gffn, wup, wdn, name):
    tm = y.shape[0]
    row = lambda i: (0, 0)
    n_groups = len(POOL_WINDOWS)
    params = [wpool, pscale, wo, gffn, wup, wdn]
    group_view = lambda g: pl.BlockSpec((tm, POOL_GROUP), lambda i: (0, g))
    return pl.pallas_call(
        _mix_sample_kernel,
        grid=(1,),
        in_specs=([pl.BlockSpec((tm, ATTN_WIDTH), row)]
                  + [group_view(g) for g in range(n_groups)]
                  + [_const_spec(hist.shape), pl.BlockSpec((tm, D_MODEL), row)]
                  + [_param_spec(a) for a in params]),
        out_specs=pl.BlockSpec((tm, D_MODEL), row),
        out_shape=jax.ShapeDtypeStruct((tm, D_MODEL), F32),
        scratch_shapes=[pltpu.VMEM((n_groups, tm, POOL_GROUP), F32)],
        compiler_params=pltpu.CompilerParams(
            dimension_semantics=("arbitrary",), vmem_limit_bytes=VMEM_LIMIT),
        name=name,
    )(attn, *([p] * n_groups), hist, y, *[_param_arg(a) for a in params])


def _mix_prompt(attn_t, p, y, wpool, pscale, wo, gffn, wup, wdn, seq, name):
    n = y.shape[0]
    tm = MIX_TILE
    assert seq % tm == 0
    seq_tiles = seq // tm
    blocks_per_tile = tm // HALO
    params = [wpool, pscale, wo, gffn, wup, wdn]
    row = lambda i: (i, 0)
    return pl.pallas_call(
        functools.partial(_mix_prompt_kernel, seq_tiles=seq_tiles),
        grid=(n // tm,),
        in_specs=[
            pl.BlockSpec((None, ATTN_WIDTH, tm), lambda i: (i // seq_tiles, 0, i % seq_tiles)),
            pl.BlockSpec((tm, POOL_WIDTH), row),
            pl.BlockSpec((HALO, POOL_WIDTH),
                         lambda i: (jnp.maximum(i * blocks_per_tile - 1, 0), 0)),
            pl.BlockSpec((tm, D_MODEL), row),
        ] + [_param_spec(a) for a in params],
        out_specs=pl.BlockSpec((tm, D_MODEL), row),
        out_shape=jax.ShapeDtypeStruct((n, D_MODEL), F32),
        scratch_shapes=[pltpu.VMEM((tm + HALO, POOL_WIDTH), F32)],
        compiler_params=pltpu.CompilerParams(
            dimension_semantics=("arbitrary",), vmem_limit_bytes=VMEM_LIMIT),
        name=name,
    )(attn_t, p, p, y, *[_param_arg(a) for a in params])


def _odd_kernel(*refs, emit_v, final_norm):
    (x_ref, gmix_ref, win_ref, lng_ref, lnb_ref, ws_ref, bs_ref, wo_ref,
     gffn_ref, wup_ref, wdn_ref) = refs[:11]
    rest = list(refs[11:])
    gfin_ref = rest.pop(0) if final_norm else None
    o_ref = rest.pop(0)
    v_ref = rest.pop(0) if emit_v else None
    tm = x_ref.shape[0]

    x = x_ref[...]
    xn = _rms(x, gmix_ref[...]).astype(BF16)
    z = jax.nn.gelu(jnp.dot(xn, win_ref[...], preferred_element_type=F32))
    u = z[:, :D_MODEL]
    v = z[:, D_MODEL:]
    mu = jnp.mean(v, axis=-1, keepdims=True)
    vc = v - mu
    var = jnp.mean(vc * vc, axis=-1, keepdims=True)
    v = vc * lax.rsqrt(var + LN_EPS) * lng_ref[...] + lnb_ref[...]
    if emit_v:
        v_ref[...] = v
    vb = v.astype(BF16)
    bias = bs_ref[...]
    chunks = []
    for c in range(tm // SGU_CHUNK):
        rows = slice(c * SGU_CHUNK, (c + 1) * SGU_CHUNK)
        cols = [jnp.dot(ws_ref[g], vb[rows, g * 128:(g + 1) * 128],
                        preferred_element_type=F32) for g in range(N_SGU_GROUPS)]
        chunks.append(jnp.concatenate(cols, axis=1) + bias)
    gated = (u * jnp.concatenate(chunks, axis=0)).astype(BF16)
    y = x + jnp.dot(gated, wo_ref[...], preferred_element_type=F32)
    y = _ffn(y, gffn_ref, wup_ref, wdn_ref)
    if final_norm:
        y = _rms(y, gfin_ref[...])
    o_ref[...] = y


def _odd_layer(x, gmix, win, lng, lnb, ws, bs, wo, gffn, wup, wdn, gfin, emit_v, name):
    n = x.shape[0]
    tm = min(ROW_TILE, n)
    row = lambda i: (i, 0)
    final_norm = gfin is not None
    params = [gmix, win, lng, lnb, ws, bs, wo, gffn, wup, wdn]
    if final_norm:
        params.append(gfin)
    args = [x] + [_param_arg(a) for a in params]
    in_specs = [pl.BlockSpec((tm, D_MODEL), row)] + [_param_spec(a) for a in params]
    out_specs = [pl.BlockSpec((tm, D_MODEL), row)]
    out_shape = [jax.ShapeDtypeStruct((n, D_MODEL), F32)]
    if emit_v:
        out_specs.append(pl.BlockSpec((tm, D_MODEL), row))
        out_shape.append(jax.ShapeDtypeStruct((n, D_MODEL), F32))
    outs = pl.pallas_call(
        functools.partial(_odd_kernel, emit_v=emit_v, final_norm=final_norm),
        grid=(n // tm,),
        in_specs=in_specs,
        out_specs=out_specs,
        out_shape=out_shape,
        compiler_params=pltpu.CompilerParams(
            dimension_semantics=("arbitrary",), vmem_limit_bytes=VMEM_LIMIT),
        name=name,
    )(*args)
    return outs if emit_v else (outs[0], None)


def _rope_tables(pos):
    inv = ROPE_THETA ** (-jnp.arange(0, HEAD_DIM, 2, dtype=F32) / HEAD_DIM)
    ang = pos.astype(F32)[:, None] * inv[None, :]
    cos, sin = jnp.cos(ang), jnp.sin(ang)
    return (jnp.concatenate([cos, cos, cos, cos], axis=1),
            jnp.concatenate([-sin, sin, -sin, sin], axis=1))


def _sgu_weights(w_s, b_s, length):
    tril = jnp.tril(jnp.ones((length, length), dtype=bool))
    w = jnp.where(tril, w_s[:, :length, :length], 0.0)
    reps = SGU_CHUNK // length
    if reps > 1:
        eye = jnp.eye(reps, dtype=w.dtype)
        w = jnp.einsum('ab,gts->gatbs', eye, w).reshape(
            N_SGU_GROUPS, SGU_CHUNK, SGU_CHUNK)
    b = jnp.tile(b_s[:, :length], (1, reps))
    b = jnp.repeat(b.T, D_MODEL // N_SGU_GROUPS, axis=1)
    return w.astype(BF16), b


def kernel(x_prompt, x_sample, cache_k, cache_v, state_pool, page_table, norm_mix, w_in_even, w_o_even, w_pool, pool_scale, w_in_odd, sgu_ln_g, sgu_ln_b, sgu_w, sgu_b, w_o_odd, norm_ffn, w_ffn_up, w_ffn_down, norm_final):
    batch, seq, _ = x_prompt.shape
    n_seq, n_new, _ = x_sample.shape
    depth = norm_mix.shape[0]
    n_phys, page = cache_k.shape[1], cache_k.shape[2]
    past_len = page_table.shape[1] * page

    yp = x_prompt.reshape(batch * seq, D_MODEL)
    ys = x_sample.reshape(n_seq * n_new, D_MODEL)
    ck = jnp.transpose(cache_k, (0, 1, 3, 4, 2)).reshape(cache_k.shape[0], n_phys, ATTN_WIDTH, page)
    cv = jnp.transpose(cache_v, (0, 1, 3, 4, 2)).reshape(cache_v.shape[0], n_phys, ATTN_WIDTH, page)

    cos_p, sin_p = _rope_tables(jnp.arange(seq))
    cos_s, sin_s = _rope_tables(past_len + jnp.arange(n_seq * n_new) % n_new)

    w_pool_b = w_pool.astype(BF16)
    w_bf16 = {("in_even", 0): w_in_even[0].astype(BF16)}
    sample_k, sample_v = [], []
    prompt_pool, sample_pool, sample_sgu = [], [], []
    n_even = (depth + 1) // 2
    kv_t = None
    for l in range(depth):
        gmix = (norm_mix[:, None, :], l)
        gffn = (norm_ffn[:, None, :], l)
        if l % 2 == 0:
            i = l // 2
            pscale = (pool_scale[:, None, :], i)
            w_in_l, w_pool_l = w_bf16["in_even", i], (w_pool_b, i)
            q, k, kt_all, vt_all, p = _inproj_even(
                yp, gmix, w_in_l, cos_p, sin_p, f"inproj_p{l}",
                kv_t_shape=(n_even, batch, ATTN_WIDTH, seq), kv_t=kv_t, layer=i)
            kv_t = (kt_all, vt_all)
            jobs = {("up", l): (w_ffn_up, l), ("dn", l): (w_ffn_down, l),
                    ("o_even", i): (w_o_even, i)}
            if l + 1 < depth:
                jobs.update({("up", l + 1): (w_ffn_up, l + 1), ("dn", l + 1): (w_ffn_down, l + 1),
                             ("in_odd", i): (w_in_odd, i), ("o_odd", i): (w_o_odd, i)})
            if i + 1 < n_even:
                jobs["in_even", i + 1] = (w_in_even, i + 1)
            attn, *cast = _attn_prompt(q, k, vt_all, i, seq, f"attn_p{l}",
                                       cast=list(jobs.values()))
            w_bf16.update(zip(jobs.keys(), cast))
            w_o_l, w_up_l, w_dn_l = w_bf16["o_even", i], w_bf16["up", l], w_bf16["dn", l]
            yp = _mix_prompt(attn, p, yp, w_pool_l, pscale, w_o_l, gffn, w_up_l, w_dn_l,
                             seq, f"mix_p{l}")
            prompt_pool.append(p.reshape(batch, seq, POOL_WIDTH)[:, seq - POOL_HIST:, :])

            q, k, v, p = _inproj_even(ys, gmix, w_in_l, cos_s, sin_s, f"inproj_s{l}")
            attn = _attn_sample(q, k, v, n_new, ck, cv, page_table, i, f"attn_s{l}")
            hist = jnp.transpose(state_pool[i], (1, 0, 2))
            ys = _mix_sample(attn, p, hist, ys, w_pool_l,
                             pscale, w_o_l, gffn, w_up_l, w_dn_l, f"mix_s{l}")
            sample_k.append(k.reshape(n_seq, n_new, N_HEADS, HEAD_DIM))
            sample_v.append(v.reshape(n_seq, n_new, N_HEADS, HEAD_DIM))
            sample_pool.append(jnp.concatenate(
                [state_pool[i], p.reshape(n_seq, n_new, POOL_WIDTH)], axis=1)[:, -POOL_HIST:, :])
        else:
            j = l // 2
            gfin = norm_final[None, :] if l == depth - 1 else None
            lng = (sgu_ln_g[:, None, :], j)
            lnb = (sgu_ln_b[:, None, :], j)
            w_in_l, w_o_l = w_bf16["in_odd", j], w_bf16["o_odd", j]
            ws_p, bs_p = _sgu_weights(sgu_w[j], sgu_b[j], SGU_CHUNK)
            ws_s, bs_s = _sgu_weights(sgu_w[j], sgu_b[j], n_new)
            w_up_l, w_dn_l = w_bf16["up", l], w_bf16["dn", l]
            yp, _ = _odd_layer(yp, gmix, w_in_l, lng, lnb, ws_p, bs_p, w_o_l,
                               gffn, w_up_l, w_dn_l, gfin, False, f"odd_p{l}")
            ys, v_rows = _odd_layer(ys, gmix, w_in_l, lng, lnb, ws_s, bs_s, w_o_l,
                                    gffn, w_up_l, w_dn_l, gfin, True, f"odd_s{l}")
            sample_sgu.append(v_rows.reshape(n_seq, n_new, D_MODEL))
    if depth % 2 == 1:
        raise NotImplementedError("the final norm is fused into an odd last layer")
    prompt_k, prompt_v = [
        jnp.transpose(t.reshape(n_even, batch, N_HEADS, HEAD_DIM, seq), (0, 1, 4, 2, 3))
        for t in kv_t]
    return (yp.reshape(batch, seq, D_MODEL), ys.reshape(n_seq, n_new, D_MODEL),
            prompt_k, prompt_v,
            jnp.stack(sample_k), jnp.stack(sample_v),
            jnp.stack(prompt_pool), jnp.stack(sample_pool),
            jnp.stack(sample_sgu))
```

```python
import functools

import jax
import jax.numpy as jnp
from jax import lax
from jax.experimental import pallas as pl
from jax.experimental.pallas import tpu as pltpu

F32 = jnp.float32
BF16 = jnp.bfloat16

D_MODEL = 1024
HEAD_DIM = 64
N_HEADS = 8
ATTN_WIDTH = 512
POOL_WIDTH = 512
POOL_WINDOWS = (2, 4, 8, 16)
POOL_GROUP = 128
POOL_HIST = 15
MOBA_BLOCK = 256
MOBA_TOPK = 3
ROPE_THETA = 10000.0
SGU_CHUNK = 128
N_SGU_GROUPS = 8
D_FF = 4096
NORM_EPS = 1e-6
LN_EPS = 1e-5

ROW_TILE = 1024
MIX_TILE = 512
FF_CHUNK = 1024
HALO = 16
MASK_BIAS = -1e30
Q_SCALE = HEAD_DIM ** -0.5 * 1.4426950408889634
V_ROWS = HEAD_DIM + 16
SAMPLE_GROUP = 2
ATTN_LEAD_UNITS = 2
VMEM_LIMIT = 56 * 1024 * 1024

_NT = (((1,), (1,)), ((), ()))


def _rms(x, g):
    ms = jnp.mean(x * x, axis=-1, keepdims=True)
    return x * lax.rsqrt(ms + NORM_EPS) * g


def _ffn(y, g_ref, wup_ref, wdn_ref):
    hn = _rms(y, g_ref[...]).astype(BF16)
    acc = y
    for c in range(D_FF // FF_CHUNK):
        sl = slice(c * FF_CHUNK, (c + 1) * FF_CHUNK)
        h = jnp.dot(hn, wup_ref[:, sl], preferred_element_type=F32)
        h = jnp.maximum(h, 0.0)
        h = (h * h).astype(BF16)
        acc = acc + jnp.dot(h, wdn_ref[sl, :], preferred_element_type=F32)
    return acc


def _round_robin(gens):
    gens = list(gens)
    while gens:
        for g in list(gens):
            if next(g, StopIteration) is StopIteration:
                gens.remove(g)


def _const_spec(shape):
    nd = len(shape)
    return pl.BlockSpec(shape, lambda *_: (0,) * nd, pipeline_mode=pl.Buffered(1))


def _param_spec(a):
    if isinstance(a, tuple):
        w, layer = a
        nd = w.ndim - 1
        return pl.BlockSpec((None,) + w.shape[1:], lambda *_: (layer,) + (0,) * nd,
                            pipeline_mode=pl.Buffered(1))
    return _const_spec(a.shape)


def _param_arg(a):
    return a[0] if isinstance(a, tuple) else a


def _cast_jobs(jobs, n_steps, flat_step):
    in_specs, args, out_specs, out_shapes = [], [], [], []
    for w, layer in jobs:
        rows, cols = w.shape[1:]
        chunk = rows // n_steps
        assert chunk * n_steps == rows and chunk % 16 == 0
        in_specs.append(pl.BlockSpec((None, chunk, cols),
                                     lambda *idx, layer=layer: (layer, flat_step(*idx), 0)))
        out_specs.append(pl.BlockSpec((chunk, cols), lambda *idx: (flat_step(*idx), 0)))
        args.append(w)
        out_shapes.append(jax.ShapeDtypeStruct((rows, cols), BF16))
    return in_specs, args, out_specs, out_shapes


def _with_cast_jobs(kernel_fn, n_in, n_out, n_jobs):
    if not n_jobs:
        return kernel_fn

    def wrapped(*refs):
        a, b, c = n_in + n_jobs, n_in + n_jobs + n_out, n_in + 2 * n_jobs + n_out
        for src, dst in zip(refs[n_in:a], refs[b:c]):
            dst[...] = src[...].astype(BF16)
        kernel_fn(*refs[:n_in], *refs[a:b], *refs[c:])
    return wrapped


def _inproj_even_kernel(x_ref, g_ref, w_ref, cos_ref, sin_ref, *refs, transposed):
    if transposed:
        q_ref, k_ref, kt_ref, vt_ref, p_ref = refs[-5:]
    else:
        q_ref, k_ref, v_ref, p_ref = refs
    xn = _rms(x_ref[...], g_ref[...]).astype(BF16)
    proj = jnp.dot(xn, w_ref[...], preferred_element_type=F32)
    cos = jnp.concatenate([cos_ref[...]] * 4, axis=1)
    sin = jnp.concatenate([sin_ref[...]] * 4, axis=1)
    lane = lax.broadcasted_iota(jnp.int32, cos.shape, 1) % HEAD_DIM
    first_half = lane < HEAD_DIM // 2

    def rot(a):
        partner = jnp.where(first_half,
                            pltpu.roll(a, ATTN_WIDTH - HEAD_DIM // 2, 1),
                            pltpu.roll(a, HEAD_DIM // 2, 1))
        return a * cos + partner * sin

    q_ref[...] = rot(proj[:, 0:ATTN_WIDTH]) * Q_SCALE
    k = rot(proj[:, ATTN_WIDTH:2 * ATTN_WIDTH])
    v = proj[:, 2 * ATTN_WIDTH:3 * ATTN_WIDTH]
    k_ref[...] = k
    if transposed:
        kt_ref[...] = k.T
        vt_ref[...] = v.T
    else:
        v_ref[...] = v
    p_ref[...] = proj[:, 3 * ATTN_WIDTH:]


def _inproj_even(x, g, w, cos, sin, name, kv_t_shape=None, kv_t=None, layer=0):
    n = x.shape[0]
    tm = min(ROW_TILE, n)
    n_pos_blocks = cos.shape[0] // tm
    row = lambda i: (i, 0)
    out = jax.ShapeDtypeStruct((n, ATTN_WIDTH), F32)
    row_spec = pl.BlockSpec((tm, ATTN_WIDTH), row)
    in_specs = [
        pl.BlockSpec((tm, D_MODEL), row),
        _param_spec(g),
        _param_spec(w),
        pl.BlockSpec((tm, 128), lambda i: (i % n_pos_blocks, 0)),
        pl.BlockSpec((tm, 128), lambda i: (i % n_pos_blocks, 0)),
    ]
    args = [x, _param_arg(g), _param_arg(w), cos, sin]
    aliases = {}
    if kv_t_shape is None:
        out_specs, out_shape = [row_spec] * 4, [out] * 4
    else:
        seq_tiles = kv_t_shape[3] // tm
        t_spec = pl.BlockSpec((None, None, ATTN_WIDTH, tm),
                              lambda i: (layer, i // seq_tiles, 0, i % seq_tiles))
        t_shape = jax.ShapeDtypeStruct(kv_t_shape, F32)
        if kv_t is not None:
            aliases = {len(args): 2, len(args) + 1: 3}
            in_specs += [pl.BlockSpec(memory_space=pl.ANY)] * 2
            args += list(kv_t)
        out_specs = [row_spec, row_spec, t_spec, t_spec, row_spec]
        out_shape = [out, out, t_shape, t_shape, out]
    return pl.pallas_call(
        functools.partial(_inproj_even_kernel, transposed=kv_t_shape is not None),
        grid=(n // tm,),
        in_specs=in_specs,
        out_specs=out_specs,
        out_shape=out_shape,
        input_output_aliases=aliases,
        compiler_params=pltpu.CompilerParams(
            dimension_semantics=("arbitrary",), vmem_limit_bytes=VMEM_LIMIT),
        name=name,
    )(*args)


def _top3_mask(gate, cand, index_f, axis=-1):
    g = jnp.where(cand, gate, -jnp.inf)
    sel = jnp.zeros(gate.shape, jnp.bool_)
    for _ in range(MOBA_TOPK):
        m = jnp.max(g, axis=axis, keepdims=True)
        idx = jnp.min(jnp.where(g == m, index_f, 1e9), axis=axis, keepdims=True)
        pick = index_f == idx
        sel = jnp.logical_or(sel, pick)
        g = jnp.where(pick, -jnp.inf, g)
    return sel


def _prompt_attn_prep(k_ref, vt_ref, kh_ref, vtb_ref, gw_ref):
    seq = k_ref.shape[0]
    n_blocks = seq // MOBA_BLOCK
    lane1 = lax.broadcasted_iota(jnp.int32, (1, 128), 1)
    lane_s = lax.broadcasted_iota(jnp.int32, (seq, 128), 1)
    k2 = k_ref[...]
    kmean = jnp.sum(k2.reshape(n_blocks, MOBA_BLOCK, 128), axis=1) * (1.0 / MOBA_BLOCK)
    for h in range(2):
        vtb_ref[h, 0:HEAD_DIM, :] = vt_ref[h * HEAD_DIM:(h + 1) * HEAD_DIM, :].astype(BF16)
        vtb_ref[h, HEAD_DIM:, :] = jnp.ones((V_ROWS - HEAD_DIM, seq), BF16)
        own = jnp.logical_and(lane_s >= h * HEAD_DIM, lane_s < (h + 1) * HEAD_DIM)
        kh_ref[h] = jnp.where(own, k2, 0.0).astype(BF16)
        own1 = jnp.logical_and(lane1 >= h * HEAD_DIM, lane1 < (h + 1) * HEAD_DIM)
        gw_ref[h] = jnp.where(own1, kmean, 0.0)


def _prompt_attn_steps(q_ref, o_ref, kh_ref, vtb_ref, gw_ref, query_blocks):
    n_blocks = q_ref.shape[0] // MOBA_BLOCK
    gate_w = [gw_ref[h] for h in range(2)]
    blk_i = lax.broadcasted_iota(jnp.int32, (n_blocks, MOBA_BLOCK), 0)
    blk_f = blk_i.astype(F32)
    key_i = lax.broadcasted_iota(jnp.int32, (MOBA_BLOCK, MOBA_BLOCK), 0)
    qry_i = lax.broadcasted_iota(jnp.int32, (MOBA_BLOCK, MOBA_BLOCK), 1)
    causal = key_i <= qry_i

    def blk(j):
        return slice(j * MOBA_BLOCK, (j + 1) * MOBA_BLOCK)

    def scores_pass(n, h, st):
        q2 = q_ref[blk(n), :]
        q2b = q2.astype(BF16)
        bias = None
        if n > MOBA_TOPK:
            gate = lax.dot_general(gate_w[h], q2, _NT, precision=lax.Precision.HIGHEST,
                                   preferred_element_type=F32)
            cand = blk_i < n
            sel = _top3_mask(gate, cand, blk_f, axis=0)
            bias = jnp.where(jnp.logical_and(cand, jnp.logical_not(sel)), MASK_BIAS, 0.0)
        m = None
        s_blocks = []
        s_t = lax.dot_general(kh_ref[h, 0:(n + 1) * MOBA_BLOCK, :], q2b, _NT,
                              preferred_element_type=F32)
        for j in range(n + 1):
            s = s_t[blk(j)]
            if j == n:
                s = jnp.where(causal, s, -jnp.inf)
            mj = jnp.max(s.reshape(MOBA_BLOCK // 8, 8, MOBA_BLOCK), axis=0)
            if bias is not None and j < n:
                mj = mj + bias[j:j + 1]
            m = mj if m is None else jnp.maximum(m, mj)
            s_blocks.append(s)
            yield
        m = jnp.max(m, axis=0, keepdims=True)
        st["s"] = s_blocks
        st["shift"] = [m if (bias is None or j == n) else m - bias[j:j + 1]
                       for j in range(n + 1)]

    def output_pass(n, h, st):
        o_t = None
        for j in range(n + 1):
            p = jnp.exp2(st["s"][j] - st["shift"][j]).astype(BF16)
            oj = jnp.dot(vtb_ref[h, :, blk(j)], p, preferred_element_type=F32)
            o_t = oj if o_t is None else o_t + oj
            yield
        o_ref[h * HEAD_DIM:(h + 1) * HEAD_DIM, blk(n)] = (
            o_t[0:HEAD_DIM] / o_t[HEAD_DIM:HEAD_DIM + 1])

    units = [(n, h) for n in query_blocks for h in range(2)]
    states = [dict() for _ in units]

    def all_scores():
        for u, unit in enumerate(units):
            yield from scores_pass(*unit, states[u])
            states[u]["ready"] = True
            yield

    def all_outputs():
        for u, unit in enumerate(units):
            ahead = min(u + ATTN_LEAD_UNITS - 1, len(units) - 1)
            while "ready" not in states[ahead]:
                yield
            yield from output_pass(*unit, states[u])

    gens = [all_scores(), all_outputs()]
    while gens:
        for g in list(gens):
            if next(g, StopIteration) is StopIteration:
                gens.remove(g)
        yield


def _attn_prompt_kernel(q_ref, k_ref, vt_ref, o_ref, kh_ref, vtb_ref, gw_ref):
    _prompt_attn_prep(k_ref, vt_ref, kh_ref, vtb_ref, gw_ref)
    n_blocks = k_ref.shape[0] // MOBA_BLOCK
    _round_robin([_prompt_attn_steps(q_ref, o_ref, kh_ref, vtb_ref, gw_ref, range(n_blocks))])


def _attn_fused_kernel(pt_ref, q_ref, k_ref, vt_ref, qs_ref, kn_ref, vn_ref, *refs,
                       n_pages, parts):
    del pt_ref
    k_pages = refs[:n_pages]
    v_pages = refs[n_pages:2 * n_pages]
    o_ref, os_ref, kh_ref, vtb_ref, gw_ref = refs[2 * n_pages:]
    c = pl.program_id(2)

    @pl.when(c == 0)
    def _():
        _prompt_attn_prep(k_ref, vt_ref, kh_ref, vtb_ref, gw_ref)

    for part, query_blocks in enumerate(parts):
        @pl.when(c == part)
        def _(query_blocks=query_blocks):
            _round_robin([
                _prompt_attn_steps(q_ref, o_ref, kh_ref, vtb_ref, gw_ref, query_blocks),
                _sample_attn_steps(qs_ref, kn_ref, vn_ref, os_ref, 0, qs_ref.shape[0],
                                   k_pages, v_pages)])


def _attn_prompt(q, k, vt_all, layer, seq, name, cast=()):
    n = q.shape[0]
    batch = n // seq
    n_pairs = ATTN_WIDTH // 128
    row_spec = pl.BlockSpec((seq, 128), lambda b, hp: (b, hp))
    c_in, c_args, c_out, c_shapes = _cast_jobs(cast, batch * n_pairs,
                                               lambda b, hp: b * n_pairs + hp)
    return pl.pallas_call(
        _with_cast_jobs(_attn_prompt_kernel, 3, 1, len(cast)),
        grid=(batch, n_pairs),
        in_specs=[row_spec, row_spec,
                  pl.BlockSpec((None, None, 128, seq), lambda b, hp: (layer, b, hp, 0))] + c_in,
        out_specs=[pl.BlockSpec((None, 128, seq), lambda b, hp: (b, hp, 0))] + c_out,
        out_shape=[jax.ShapeDtypeStruct((batch, ATTN_WIDTH, seq), F32)] + c_shapes,
        scratch_shapes=_prompt_attn_scratch(seq),
        compiler_params=pltpu.CompilerParams(
            dimension_semantics=("arbitrary", "arbitrary"), vmem_limit_bytes=VMEM_LIMIT),
        name=name,
    )(q, k, vt_all, *c_args)


def _prompt_attn_scratch(seq):
    return [pltpu.VMEM((2, seq, 128), BF16), pltpu.VMEM((2, V_ROWS, seq), BF16),
            pltpu.VMEM((2, seq // MOBA_BLOCK, 128), F32)]


def _attn_fused(q, k, vt_all, layer, seq, q_s, k_new, v_new, cache_k, cache_v, page_table,
                name, cast=()):
    n = q.shape[0]
    batch = n // seq
    n_pairs = ATTN_WIDTH // 128
    n_seq, n_new, _ = q_s.shape
    n_parts = n_seq // (batch * n_pairs)
    parts = _split_query_blocks(seq // MOBA_BLOCK, n_parts)
    assert n_seq == batch * n_pairs * n_parts
    n_pages = page_table.shape[1]
    page = cache_k.shape[3]
    step = lambda b, hp, c: (b * n_pairs + hp) * n_parts + c

    row_spec = pl.BlockSpec((seq, 128), lambda b, hp, c, pt: (b, hp))
    sample_spec = pl.BlockSpec((None, n_new, ATTN_WIDTH), lambda b, hp, c, pt: (step(b, hp, c), 0, 0))

    def page_spec(j):
        return pl.BlockSpec((None, None, ATTN_WIDTH, page),
                            lambda b, hp, c, pt: (layer, pt[step(b, hp, c), j], 0, 0))

    c_in, c_args, c_out, c_shapes = _cast_jobs(cast, batch * n_pairs,
                                               lambda b, hp, c, pt: b * n_pairs + hp)
    n_in = 1 + 6 + 2 * n_pages
    kernel_fn = functools.partial(_attn_fused_kernel, n_pages=n_pages, parts=parts)
    grid_spec = pltpu.PrefetchScalarGridSpec(
        num_scalar_prefetch=1,
        grid=(batch, n_pairs, n_parts),
        in_specs=[row_spec, row_spec,
                  pl.BlockSpec((None, None, 128, seq), lambda b, hp, c, pt: (layer, b, hp, 0)),
                  sample_spec, sample_spec, sample_spec]
                 + [page_spec(j) for j in range(n_pages)] * 2 + c_in,
        out_specs=[pl.BlockSpec((None, 128, seq), lambda b, hp, c, pt: (b, hp, 0)),
                   sample_spec] + c_out,
        scratch_shapes=_prompt_attn_scratch(seq),
    )
    return pl.pallas_call(
        _with_cast_jobs(kernel_fn, n_in, 2, len(cast)),
        grid_spec=grid_spec,
        out_shape=[jax.ShapeDtypeStruct((batch, ATTN_WIDTH, seq), F32),
                   jax.ShapeDtypeStruct((n_seq, n_new, ATTN_WIDTH), F32)] + c_shapes,
        compiler_params=pltpu.CompilerParams(
            dimension_semantics=("arbitrary", "arbitrary", "arbitrary"),
            vmem_limit_bytes=VMEM_LIMIT),
        name=name,
    )(page_table, q, k, vt_all, q_s, k_new, v_new,
      *([cache_k] * n_pages), *([cache_v] * n_pages), *c_args)


def _split_query_blocks(n_blocks, n_parts):
    total = n_blocks * (n_blocks + 1) // 2
    parts, cur, done = [], [], 0
    for n in range(n_blocks):
        cur.append(n)
        done += n + 1
        if done * n_parts >= total * (len(parts) + 1) and len(parts) < n_parts - 1:
            parts.append(cur)
            cur = []
    parts.append(cur)
    assert len(parts) == n_parts and all(parts)
    return parts


def _sample_attn_steps(q_ref, kn_ref, vn_ref, o_ref, row0, n_new, k_pages, v_pages):
    n_pages = len(k_pages)
    page = k_pages[0].shape[1]
    pages_per_block = MOBA_BLOCK // page
    n_blocks = n_pages // pages_per_block
    n_rows = n_new * N_HEADS

    lane = lax.broadcasted_iota(jnp.int32, (N_HEADS, ATTN_WIDTH), 1)
    head = lax.broadcasted_iota(jnp.int32, (N_HEADS, ATTN_WIDTH), 0)
    head_lanes = (lane // HEAD_DIM == head).astype(F32)
    row_t = lax.broadcasted_iota(jnp.int32, (n_rows, 1), 0) // N_HEADS
    lane_w = lax.broadcasted_iota(jnp.int32, (ATTN_WIDTH, 128), 1)
    lane_g = lax.broadcasted_iota(jnp.int32, (n_rows, 128), 1)
    lane_gf = lane_g.astype(F32)

    def block_of(pages, j):
        return jnp.concatenate([pages[pages_per_block * j + i][...]
                                for i in range(pages_per_block)], axis=1)

    q_rows = [q_ref[row0 + t:row0 + t + 1, :] * head_lanes for t in range(n_new)]
    qbd = jnp.concatenate(q_rows, axis=0)
    qbd_b = qbd.astype(BF16)
    gate_w = jnp.zeros((ATTN_WIDTH, 128), F32)
    scores = []
    for j in range(n_blocks):
        kj = block_of(k_pages, j)
        mean_j = jnp.sum(kj, axis=1, keepdims=True) * (1.0 / MOBA_BLOCK)
        gate_w = jnp.where(lane_w == j, mean_j, gate_w)
        scores.append(jnp.dot(qbd_b, kj.astype(BF16), preferred_element_type=F32))
        yield

    gate = jnp.dot(qbd, gate_w, precision=lax.Precision.HIGHEST,
                   preferred_element_type=F32)
    sel = _top3_mask(gate, lane_g < n_blocks, lane_gf).astype(F32)
    yield

    m = jnp.full((n_rows, 1), -jnp.inf, F32)
    for j in range(n_blocks):
        sel_j = jnp.sum(jnp.where(lane_g == j, sel, 0.0), axis=-1, keepdims=True) > 0.5
        scores[j] = jnp.where(sel_j, scores[j], -jnp.inf)
        m = jnp.maximum(m, jnp.max(scores[j], axis=-1, keepdims=True))
    s_new = []
    for t in range(n_new):
        sc = jnp.sum(qbd * kn_ref[row0 + t:row0 + t + 1, :], axis=-1, keepdims=True)
        sc = jnp.where(row_t >= t, sc, -jnp.inf)
        s_new.append(sc)
        m = jnp.maximum(m, sc)
    yield

    l = jnp.zeros((n_rows, 1), F32)
    acc = jnp.zeros((n_rows, ATTN_WIDTH), F32)
    for j in range(n_blocks):
        p = jnp.exp2(scores[j] - m)
        l = l + jnp.sum(p, axis=-1, keepdims=True)
        acc = acc + lax.dot_general(p.astype(BF16), block_of(v_pages, j).astype(BF16), _NT,
                                    preferred_element_type=F32)
        yield
    for t in range(n_new):
        p = jnp.exp2(s_new[t] - m)
        l = l + p
        acc = acc + p * vn_ref[row0 + t:row0 + t + 1, :]

    acc = acc / l * jnp.concatenate([head_lanes] * n_new, axis=0)
    o_ref[row0:row0 + n_new, :] = jnp.sum(acc.reshape(n_new, N_HEADS, ATTN_WIDTH), axis=1)


def _attn_sample_kernel(pt_ref, q_ref, kn_ref, vn_ref, *refs, n_pages, group):
    del pt_ref
    o_ref = refs[2 * group * n_pages]
    n_new = q_ref.shape[0] // group
    _round_robin(
        _sample_attn_steps(q_ref, kn_ref, vn_ref, o_ref, s * n_new, n_new,
                           refs[s * n_pages:(s + 1) * n_pages],
                           refs[(group + s) * n_pages:(group + s + 1) * n_pages])
        for s in range(group))


def _attn_sample(q, k_new, v_new, n_new, cache_k, cache_v, page_table, layer, name):
    n_seq = q.shape[0] // n_new
    n_pages = page_table.shape[1]
    page = cache_k.shape[3]
    group = SAMPLE_GROUP
    new_spec = pl.BlockSpec((group * n_new, ATTN_WIDTH), lambda a, pt: (a, 0))

    def page_spec(s, j):
        return pl.BlockSpec((None, None, ATTN_WIDTH, page),
                            lambda a, pt: (layer, pt[a * group + s, j], 0, 0))

    page_specs = [page_spec(s, j) for s in range(group) for j in range(n_pages)]
    grid_spec = pltpu.PrefetchScalarGridSpec(
        num_scalar_prefetch=1,
        grid=(n_seq // group,),
        in_specs=[new_spec] * 3 + page_specs * 2,
        out_specs=new_spec,
    )
    n_page_args = group * n_pages
    return pl.pallas_call(
        functools.partial(_attn_sample_kernel, n_pages=n_pages, group=group),
        grid_spec=grid_spec,
        out_shape=jax.ShapeDtypeStruct((n_seq * n_new, ATTN_WIDTH), F32),
        compiler_params=pltpu.CompilerParams(
            dimension_semantics=("arbitrary",), vmem_limit_bytes=VMEM_LIMIT),
        name=name,
    )(page_table, q, k_new, v_new, *([cache_k] * n_page_args), *([cache_v] * n_page_args))


def _pool_windows(read, tm, count):
    groups = []
    for g, w in enumerate(POOL_WINDOWS):
        cur = read(0, g)
        total = cur
        for back in range(1, w):
            total = total + read(back, g)
        groups.append(total / count(w) - cur)
    return groups


def _mix_residual(groups, attn, y, wpool_ref, pscale_ref, wo_ref):
    pooled = [jnp.dot(groups[g].astype(BF16), wpool_ref[g], preferred_element_type=F32)
              for g in range(len(POOL_WINDOWS))]
    pooled = jnp.concatenate(pooled, axis=1) * pscale_ref[...]
    mix = jnp.concatenate([attn, pooled], axis=1).astype(BF16)
    return y + jnp.dot(mix, wo_ref[...], preferred_element_type=F32)


def _mix_sample_kernel(*refs):
    n_groups = len(POOL_WINDOWS)
    attn_ref = refs[0]
    p_refs = refs[1:1 + n_groups]
    (hist_ref, y_ref, wpool_ref, pscale_ref, wo_ref, gffn_ref, wup_ref, wdn_ref,
     o_ref, pooled_ref) = refs[1 + n_groups:]
    tm = y_ref.shape[0]
    n_seq = hist_ref.shape[1]
    n_new = tm // n_seq
    for t in range(n_new):
        def read(back, g, t=t):
            if back <= t:
                return p_refs[g][pl.ds(t - back, n_seq, stride=n_new), :]
            return hist_ref[POOL_HIST + t - back, :, g * POOL_GROUP:(g + 1) * POOL_GROUP]
        groups = _pool_windows(read, n_seq, lambda w: float(w))
        for g in range(n_groups):
            pooled_ref[g, pl.ds(t, n_seq, stride=n_new), :] = groups[g]
    groups = [pooled_ref[g] for g in range(n_groups)]
    y = _mix_residual(groups, attn_ref[...], y_ref[...], wpool_ref, pscale_ref, wo_ref)
    o_ref[...] = _ffn(y, gffn_ref, wup_ref, wdn_ref)


def _mix_prompt_kernel(attn_ref, p_ref, halo_ref, y_ref, wpool_ref, pscale_ref, wo_ref,
                       gffn_ref, wup_ref, wdn_ref, o_ref, xp_ref, *, seq_tiles):
    tm = p_ref.shape[0]
    seq_pos0 = (pl.program_id(0) % seq_tiles) * tm
    xp_ref[0:HALO, :] = jnp.where(seq_pos0 == 0, 0.0, halo_ref[...])
    xp_ref[HALO:, :] = p_ref[...]
    pos1 = (seq_pos0 + lax.broadcasted_iota(jnp.int32, (tm, 1), 0) + 1).astype(F32)

    def read(back, g):
        return xp_ref[pl.ds(HALO - back, tm), g * POOL_GROUP:(g + 1) * POOL_GROUP]
    groups = _pool_windows(read, tm, lambda w: jnp.minimum(pos1, float(w)))
    y = _mix_residual(groups, attn_ref[...].T, y_ref[...], wpool_ref, pscale_ref, wo_ref)
    o_ref[...] = _ffn(y, gffn_ref, wup_ref, wdn_ref)


def _mix_sample(attn, p, hist, y, wpool, pscale, wo, gffn, wup, wdn, name):
    tm = y.shape[0]
    row = lambda i: (0, 0)
    n_groups = len(POOL_WINDOWS)
    params = [wpool, pscale, wo, gffn, wup, wdn]
    group_view = lambda g: pl.BlockSpec((tm, POOL_GROUP), lambda i: (0, g))
    return pl.pallas_call(
        _mix_sample_kernel,
        grid=(1,),
        in_specs=([pl.BlockSpec((tm, ATTN_WIDTH), row)]
                  + [group_view(g) for g in range(n_groups)]
                  + [_const_spec(hist.shape), pl.BlockSpec((tm, D_MODEL), row)]
                  + [_param_spec(a) for a in params]),
        out_specs=pl.BlockSpec((tm, D_MODEL), row),
        out_shape=jax.ShapeDtypeStruct((tm, D_MODEL), F32),
        scratch_shapes=[pltpu.VMEM((n_groups, tm, POOL_GROUP), F32)],
        compiler_params=pltpu.CompilerParams(
            dimension_semantics=("arbitrary",), vmem_limit_bytes=VMEM_LIMIT),
        name=name,
    )(attn, *([p] * n_groups), hist, y, *[_param_arg(a) for a in params])


def _mix_prompt(attn_t, p, y, wpool, pscale, wo, gffn, wup, wdn, seq, name):
    n = y.shape[0]
    tm = MIX_TILE
    assert seq % tm == 0
    seq_tiles = seq // tm
    blocks_per_tile = tm // HALO
    params = [wpool, pscale, wo, gffn, wup, wdn]
    row = lambda i: (i, 0)
    return pl.pallas_call(
        functools.partial(_mix_prompt_kernel, seq_tiles=seq_tiles),
        grid=(n // tm,),
        in_specs=[
            pl.BlockSpec((None, ATTN_WIDTH, tm), lambda i: (i // seq_tiles, 0, i % seq_tiles)),
            pl.BlockSpec((tm, POOL_WIDTH), row),
            pl.BlockSpec((HALO, POOL_WIDTH),
                         lambda i: (jnp.maximum(i * blocks_per_tile - 1, 0), 0)),
            pl.BlockSpec((tm, D_MODEL), row),
        ] + [_param_spec(a) for a in params],
        out_specs=pl.BlockSpec((tm, D_MODEL), row),
        out_shape=jax.ShapeDtypeStruct((n, D_MODEL), F32),
        scratch_shapes=[pltpu.VMEM((tm + HALO, POOL_WIDTH), F32)],
        compiler_params=pltpu.CompilerParams(
            dimension_semantics=("arbitrary",), vmem_limit_bytes=VMEM_LIMIT),
        name=name,
    )(attn_t, p, p, y, *[_param_arg(a) for a in params])


def _odd_kernel(*refs, emit_v, final_norm):
    (x_ref, gmix_ref, win_ref, lng_ref, lnb_ref, ws_ref, bs_ref, wo_ref,
     gffn_ref, wup_ref, wdn_ref) = refs[:11]
    rest = list(refs[11:])
    gfin_ref = rest.pop(0) if final_norm else None
    o_ref = rest.pop(0)
    v_ref = rest.pop(0) if emit_v else None
    tm = x_ref.shape[0]

    x = x_ref[...]
    xn = _rms(x, gmix_ref[...]).astype(BF16)
    z = jax.nn.gelu(jnp.dot(xn, win_ref[...], preferred_element_type=F32))
    u = z[:, :D_MODEL]
    v = z[:, D_MODEL:]
    mu = jnp.mean(v, axis=-1, keepdims=True)
    vc = v - mu
    var = jnp.mean(vc * vc, axis=-1, keepdims=True)
    v = vc * lax.rsqrt(var + LN_EPS) * lng_ref[...] + lnb_ref[...]
    if emit_v:
        v_ref[...] = v
    vb = v.astype(BF16)
    bias = bs_ref[...]
    chunks = []
    for c in range(tm // SGU_CHUNK):
        rows = slice(c * SGU_CHUNK, (c + 1) * SGU_CHUNK)
        cols = [jnp.dot(ws_ref[g], vb[rows, g * 128:(g + 1) * 128],
                        preferred_element_type=F32) for g in range(N_SGU_GROUPS)]
        chunks.append(jnp.concatenate(cols, axis=1) + bias)
    gated = (u * jnp.concatenate(chunks, axis=0)).astype(BF16)
    y = x + jnp.dot(gated, wo_ref[...], preferred_element_type=F32)
    y = _ffn(y, gffn_ref, wup_ref, wdn_ref)
    if final_norm:
        y = _rms(y, gfin_ref[...])
    o_ref[...] = y


def _odd_layer(x, gmix, win, lng, lnb, ws, bs, wo, gffn, wup, wdn, gfin, emit_v, name):
    n = x.shape[0]
    tm = min(ROW_TILE, n)
    row = lambda i: (i, 0)
    final_norm = gfin is not None
    params = [gmix, win, lng, lnb, ws, bs, wo, gffn, wup, wdn]
    if final_norm:
        params.append(gfin)
    args = [x] + [_param_arg(a) for a in params]
    in_specs = [pl.BlockSpec((tm, D_MODEL), row)] + [_param_spec(a) for a in params]
    out_specs = [pl.BlockSpec((tm, D_MODEL), row)]
    out_shape = [jax.ShapeDtypeStruct((n, D_MODEL), F32)]
    if emit_v:
        out_specs.append(pl.BlockSpec((tm, D_MODEL), row))
        out_shape.append(jax.ShapeDtypeStruct((n, D_MODEL), F32))
    outs = pl.pallas_call(
        functools.partial(_odd_kernel, emit_v=emit_v, final_norm=final_norm),
        grid=(n // tm,),
        in_specs=in_specs,
        out_specs=out_specs,
        out_shape=out_shape,
        compiler_params=pltpu.CompilerParams(
            dimension_semantics=("arbitrary",), vmem_limit_bytes=VMEM_LIMIT),
        name=name,
    )(*args)
    return outs if emit_v else (outs[0], None)


def _rope_tables(pos):
    inv = ROPE_THETA ** (-jnp.arange(0, HEAD_DIM, 2, dtype=F32) / HEAD_DIM)
    ang = pos.astype(F32)[:, None] * inv[None, :]
    cos, sin = jnp.cos(ang), jnp.sin(ang)
    return (jnp.concatenate([cos, cos, cos, cos], axis=1),
            jnp.concatenate([-sin, sin, -sin, sin], axis=1))


def _sgu_weights(w_s, b_s, length):
    tril = jnp.tril(jnp.ones((length, length), dtype=bool))
    w = jnp.where(tril, w_s[:, :length, :length], 0.0)
    reps = SGU_CHUNK // length
    if reps > 1:
        eye = jnp.eye(reps, dtype=w.dtype)
        w = jnp.einsum('ab,gts->gatbs', eye, w).reshape(
            N_SGU_GROUPS, SGU_CHUNK, SGU_CHUNK)
    b = jnp.tile(b_s[:, :length], (1, reps))
    b = jnp.repeat(b.T, D_MODEL // N_SGU_GROUPS, axis=1)
    return w.astype(BF16), b


def kernel(x_prompt, x_sample, cache_k, cache_v, state_pool, page_table, norm_mix, w_in_even, w_o_even, w_pool, pool_scale, w_in_odd, sgu_ln_g, sgu_ln_b, sgu_w, sgu_b, w_o_odd, norm_ffn, w_ffn_up, w_ffn_down, norm_final):
    batch, seq, _ = x_prompt.shape
    n_seq, n_new, _ = x_sample.shape
    depth = norm_mix.shape[0]
    n_phys, page = cache_k.shape[1], cache_k.shape[2]
    past_len = page_table.shape[1] * page

    yp = x_prompt.reshape(batch * seq, D_MODEL)
    ys = x_sample.reshape(n_seq * n_new, D_MODEL)
    ck = jnp.transpose(cache_k, (0, 1, 3, 4, 2)).reshape(cache_k.shape[0], n_phys, ATTN_WIDTH, page)
    cv = jnp.transpose(cache_v, (0, 1, 3, 4, 2)).reshape(cache_v.shape[0], n_phys, ATTN_WIDTH, page)

    cos_p, sin_p = _rope_tables(jnp.arange(seq))
    cos_s, sin_s = _rope_tables(past_len + jnp.arange(n_seq * n_new) % n_new)

    w_pool_b = w_pool.astype(BF16)
    w_bf16 = {("in_even", 0): w_in_even[0].astype(BF16)}
    sample_k, sample_v = [], []
    prompt_pool, sample_pool, sample_sgu = [], [], []
    n_even = (depth + 1) // 2
    kv_t = None
    for l in range(depth):
        gmix = (norm_mix[:, None, :], l)
        gffn = (norm_ffn[:, None, :], l)
        if l % 2 == 0:
            i = l // 2
            pscale = (pool_scale[:, None, :], i)
            w_in_l, w_pool_l = w_bf16["in_even", i], (w_pool_b, i)
            q, k, kt_all, vt_all, p = _inproj_even(
                yp, gmix, w_in_l, cos_p, sin_p, f"inproj_p{l}",
                kv_t_shape=(n_even, batch, ATTN_WIDTH, seq), kv_t=kv_t, layer=i)
            kv_t = (kt_all, vt_all)
            jobs = {("up", l): (w_ffn_up, l), ("dn", l): (w_ffn_down, l),
                    ("o_even", i): (w_o_even, i)}
            if l + 1 < depth:
                jobs.update({("up", l + 1): (w_ffn_up, l + 1), ("dn", l + 1): (w_ffn_down, l + 1),
                             ("in_odd", i): (w_in_odd, i), ("o_odd", i): (w_o_odd, i)})
            if i + 1 < n_even:
                jobs["in_even", i + 1] = (w_in_even, i + 1)
            k_prompt, p_prompt = k, p
            q_s, k, v, p = _inproj_even(ys, gmix, w_in_l, cos_s, sin_s, f"inproj_s{l}")
            shape3 = (n_seq, n_new, ATTN_WIDTH)
            attn, attn_s, *cast = _attn_fused(
                q, k_prompt, vt_all, i, seq, q_s.reshape(shape3), k.reshape(shape3),
                v.reshape(shape3), ck, cv, page_table, f"attn{l}", cast=list(jobs.values()))
            w_bf16.update(zip(jobs.keys(), cast))
            w_o_l, w_up_l, w_dn_l = w_bf16["o_even", i], w_bf16["up", l], w_bf16["dn", l]
            yp = _mix_prompt(attn, p_prompt, yp, w_pool_l, pscale, w_o_l, gffn, w_up_l, w_dn_l,
                             seq, f"mix_p{l}")
            prompt_pool.append(p_prompt.reshape(batch, seq, POOL_WIDTH)[:, seq - POOL_HIST:, :])

            attn = attn_s.reshape(n_seq * n_new, ATTN_WIDTH)
            hist = jnp.transpose(state_pool[i], (1, 0, 2))
            ys = _mix_sample(attn, p, hist, ys, w_pool_l,
                             pscale, w_o_l, gffn, w_up_l, w_dn_l, f"mix_s{l}")
            sample_k.append(k.reshape(n_seq, n_new, N_HEADS, HEAD_DIM))
            sample_v.append(v.reshape(n_seq, n_new, N_HEADS, HEAD_DIM))
            sample_pool.append(jnp.concatenate(
                [state_pool[i], p.reshape(n_seq, n_new, POOL_WIDTH)], axis=1)[:, -POOL_HIST:, :])
        else:
            j = l // 2
            gfin = norm_final[None, :] if l == depth - 1 else None
            lng = (sgu_ln_g[:, None, :], j)
            lnb = (sgu_ln_b[:, None, :], j)
            w_in_l, w_o_l = w_bf16["in_odd", j], w_bf16["o_odd", j]
            ws_p, bs_p = _sgu_weights(sgu_w[j], sgu_b[j], SGU_CHUNK)
            ws_s, bs_s = _sgu_weights(sgu_w[j], sgu_b[j], n_new)
            w_up_l, w_dn_l = w_bf16["up", l], w_bf16["dn", l]
            yp, _ = _odd_layer(yp, gmix, w_in_l, lng, lnb, ws_p, bs_p, w_o_l,
                               gffn, w_up_l, w_dn_l, gfin, False, f"odd_p{l}")
            ys, v_rows = _odd_layer(ys, gmix, w_in_l, lng, lnb, ws_s, bs_s, w_o_l,
                                    gffn, w_up_l, w_dn_l, gfin, True, f"odd_s{l}")
            sample_sgu.append(v_rows.reshape(n_seq, n_new, D_MODEL))
    if depth % 2 == 1:
        raise NotImplementedError("the final norm is fused into an odd last layer")
    prompt_k, prompt_v = [
        jnp.transpose(t.reshape(n_even, batch, N_HEADS, HEAD_DIM, seq), (0, 1, 4, 2, 3))
        for t in kv_t]
    return (yp.reshape(batch, seq, D_MODEL), ys.reshape(n_seq, n_new, D_MODEL),
            prompt_k, prompt_v,
            jnp.stack(sample_k), jnp.stack(sample_v),
            jnp.stack(prompt_pool), jnp.stack(sample_pool),
            jnp.stack(sample_sgu))
```

```python
import functools

import jax
import jax.numpy as jnp
from jax import lax
from jax.experimental import pallas as pl
from jax.experimental.pallas import tpu as pltpu

F32 = jnp.float32
BF16 = jnp.bfloat16

D_MODEL = 1024
HEAD_DIM = 64
N_HEADS = 8
ATTN_WIDTH = 512
POOL_WIDTH = 512
POOL_WINDOWS = (2, 4, 8, 16)
POOL_GROUP = 128
POOL_HIST = 15
MOBA_BLOCK = 256
MOBA_TOPK = 3
ROPE_THETA = 10000.0
SGU_CHUNK = 128
N_SGU_GROUPS = 8
D_FF = 4096
NORM_EPS = 1e-6
LN_EPS = 1e-5

ROW_TILE = 1024
MIX_TILE = 512
FF_CHUNK = 1024
HALO = 16
MASK_BIAS = -1e30
Q_SCALE = HEAD_DIM ** -0.5 * 1.4426950408889634
V_ROWS = HEAD_DIM + 16
SAMPLE_GROUP = 2
ATTN_LEAD_UNITS = 2
VMEM_LIMIT = 56 * 1024 * 1024

_NT = (((1,), (1,)), ((), ()))


def _rms(x, g):
    ms = jnp.mean(x * x, axis=-1, keepdims=True)
    return x * lax.rsqrt(ms + NORM_EPS) * g


def _ffn(y, g_ref, wup_ref, wdn_ref):
    hn = _rms(y, g_ref[...]).astype(BF16)
    acc = y
    for c in range(D_FF // FF_CHUNK):
        sl = slice(c * FF_CHUNK, (c + 1) * FF_CHUNK)
        h = jnp.dot(hn, wup_ref[:, sl], preferred_element_type=F32)
        h = jnp.maximum(h, 0.0)
        h = (h * h).astype(BF16)
        acc = acc + jnp.dot(h, wdn_ref[sl, :], preferred_element_type=F32)
    return acc


def _round_robin(gens):
    gens = list(gens)
    while gens:
        for g in list(gens):
            if next(g, StopIteration) is StopIteration:
                gens.remove(g)


def _const_spec(shape):
    nd = len(shape)
    return pl.BlockSpec(shape, lambda *_: (0,) * nd, pipeline_mode=pl.Buffered(1))


def _param_spec(a):
    if isinstance(a, tuple):
        w, layer = a
        nd = w.ndim - 1
        return pl.BlockSpec((None,) + w.shape[1:], lambda *_: (layer,) + (0,) * nd,
                            pipeline_mode=pl.Buffered(1))
    return _const_spec(a.shape)


def _param_arg(a):
    return a[0] if isinstance(a, tuple) else a


def _cast_jobs(jobs, n_steps, flat_step):
    in_specs, args, out_specs, out_shapes = [], [], [], []
    for w, layer in jobs:
        rows, cols = w.shape[1:]
        chunk = rows // n_steps
        assert chunk * n_steps == rows and chunk % 16 == 0
        in_specs.append(pl.BlockSpec((None, chunk, cols),
                                     lambda *idx, layer=layer: (layer, flat_step(*idx), 0)))
        out_specs.append(pl.BlockSpec((chunk, cols), lambda *idx: (flat_step(*idx), 0)))
        args.append(w)
        out_shapes.append(jax.ShapeDtypeStruct((rows, cols), BF16))
    return in_specs, args, out_specs, out_shapes


def _with_cast_jobs(kernel_fn, n_in, n_out, n_jobs):
    if not n_jobs:
        return kernel_fn

    def wrapped(*refs):
        a, b, c = n_in + n_jobs, n_in + n_jobs + n_out, n_in + 2 * n_jobs + n_out
        for src, dst in zip(refs[n_in:a], refs[b:c]):
            dst[...] = src[...].astype(BF16)
        kernel_fn(*refs[:n_in], *refs[a:b], *refs[c:])
    return wrapped


def _inproj_even_kernel(x_ref, g_ref, w_ref, cos_ref, sin_ref, *refs, transposed):
    if transposed:
        q_ref, k_ref, kt_ref, vt_ref, p_ref = refs[-5:]
    else:
        q_ref, k_ref, v_ref, p_ref = refs
    xn = _rms(x_ref[...], g_ref[...]).astype(BF16)
    proj = jnp.dot(xn, w_ref[...], preferred_element_type=F32)
    cos = jnp.concatenate([cos_ref[...]] * 4, axis=1)
    sin = jnp.concatenate([sin_ref[...]] * 4, axis=1)
    lane = lax.broadcasted_iota(jnp.int32, cos.shape, 1) % HEAD_DIM
    first_half = lane < HEAD_DIM // 2

    def rot(a):
        partner = jnp.where(first_half,
                            pltpu.roll(a, ATTN_WIDTH - HEAD_DIM // 2, 1),
                            pltpu.roll(a, HEAD_DIM // 2, 1))
        return a * cos + partner * sin

    q = rot(proj[:, 0:ATTN_WIDTH]) * Q_SCALE
    k = rot(proj[:, ATTN_WIDTH:2 * ATTN_WIDTH])
    v = proj[:, 2 * ATTN_WIDTH:3 * ATTN_WIDTH]
    if transposed:
        for hp in range(ATTN_WIDTH // 128):
            q_ref[hp] = q[:, hp * 128:(hp + 1) * 128]
            k_ref[hp] = k[:, hp * 128:(hp + 1) * 128]
        kt_ref[...] = k.T
        vt_ref[...] = v.T
    else:
        q_ref[...] = q
        k_ref[...] = k
        v_ref[...] = v
    p_ref[...] = proj[:, 3 * ATTN_WIDTH:]


def _inproj_even(x, g, w, cos, sin, name, kv_t_shape=None, kv_t=None, layer=0):
    n = x.shape[0]
    tm = min(ROW_TILE, n)
    n_pos_blocks = cos.shape[0] // tm
    row = lambda i: (i, 0)
    out = jax.ShapeDtypeStruct((n, ATTN_WIDTH), F32)
    row_spec = pl.BlockSpec((tm, ATTN_WIDTH), row)
    in_specs = [
        pl.BlockSpec((tm, D_MODEL), row),
        _param_spec(g),
        _param_spec(w),
        pl.BlockSpec((tm, 128), lambda i: (i % n_pos_blocks, 0)),
        pl.BlockSpec((tm, 128), lambda i: (i % n_pos_blocks, 0)),
    ]
    args = [x, _param_arg(g), _param_arg(w), cos, sin]
    aliases = {}
    if kv_t_shape is None:
        out_specs, out_shape = [row_spec] * 4, [out] * 4
    else:
        seq_tiles = kv_t_shape[3] // tm
        t_spec = pl.BlockSpec((None, None, ATTN_WIDTH, tm),
                              lambda i: (layer, i // seq_tiles, 0, i % seq_tiles))
        t_shape = jax.ShapeDtypeStruct(kv_t_shape, F32)
        if kv_t is not None:
            aliases = {len(args): 2, len(args) + 1: 3}
            in_specs += [pl.BlockSpec(memory_space=pl.ANY)] * 2
            args += list(kv_t)
        n_pairs = ATTN_WIDTH // 128
        pair_spec = pl.BlockSpec((n_pairs, tm, 128), lambda i: (0, i, 0))
        pair_shape = jax.ShapeDtypeStruct((n_pairs, n, 128), F32)
        out_specs = [pair_spec, pair_spec, t_spec, t_spec, row_spec]
        out_shape = [pair_shape, pair_shape, t_shape, t_shape, out]
    return pl.pallas_call(
        functools.partial(_inproj_even_kernel, transposed=kv_t_shape is not None),
        grid=(n // tm,),
        in_specs=in_specs,
        out_specs=out_specs,
        out_shape=out_shape,
        input_output_aliases=aliases,
        compiler_params=pltpu.CompilerParams(
            dimension_semantics=("arbitrary",), vmem_limit_bytes=VMEM_LIMIT),
        name=name,
    )(*args)


def _top3_mask(gate, cand, index_f, axis=-1):
    g = jnp.where(cand, gate, -jnp.inf)
    sel = jnp.zeros(gate.shape, jnp.bool_)
    for _ in range(MOBA_TOPK):
        m = jnp.max(g, axis=axis, keepdims=True)
        idx = jnp.min(jnp.where(g == m, index_f, 1e9), axis=axis, keepdims=True)
        pick = index_f == idx
        sel = jnp.logical_or(sel, pick)
        g = jnp.where(pick, -jnp.inf, g)
    return sel


def _attn_prompt_kernel(q_ref, k_ref, vt_ref, o_ref, kh_ref, vtb_ref):
    seq = k_ref.shape[0]
    n_blocks = seq // MOBA_BLOCK
    lane1 = lax.broadcasted_iota(jnp.int32, (1, 128), 1)
    lane_s = lax.broadcasted_iota(jnp.int32, (seq, 128), 1)
    k2 = k_ref[...]
    kmean = jnp.sum(k2.reshape(n_blocks, MOBA_BLOCK, 128), axis=1) * (1.0 / MOBA_BLOCK)
    gate_w = []
    for h in range(2):
        vtb_ref[h, 0:HEAD_DIM, :] = vt_ref[h * HEAD_DIM:(h + 1) * HEAD_DIM, :].astype(BF16)
        vtb_ref[h, HEAD_DIM:, :] = jnp.ones((V_ROWS - HEAD_DIM, seq), BF16)
        own = jnp.logical_and(lane_s >= h * HEAD_DIM, lane_s < (h + 1) * HEAD_DIM)
        kh_ref[h] = jnp.where(own, k2, 0.0).astype(BF16)
        own1 = jnp.logical_and(lane1 >= h * HEAD_DIM, lane1 < (h + 1) * HEAD_DIM)
        gate_w.append(jnp.where(own1, kmean, 0.0))

    blk_i = lax.broadcasted_iota(jnp.int32, (n_blocks, MOBA_BLOCK), 0)
    blk_f = blk_i.astype(F32)
    key_i = lax.broadcasted_iota(jnp.int32, (MOBA_BLOCK, MOBA_BLOCK), 0)
    qry_i = lax.broadcasted_iota(jnp.int32, (MOBA_BLOCK, MOBA_BLOCK), 1)
    causal = key_i <= qry_i

    def blk(j):
        return slice(j * MOBA_BLOCK, (j + 1) * MOBA_BLOCK)

    def scores_pass(n, h, st):
        q2 = q_ref[blk(n), :]
        q2b = q2.astype(BF16)
        bias = None
        if n > MOBA_TOPK:
            gate = lax.dot_general(gate_w[h], q2, _NT, precision=lax.Precision.HIGHEST,
                                   preferred_element_type=F32)
            cand = blk_i < n
            sel = _top3_mask(gate, cand, blk_f, axis=0)
            bias = jnp.where(jnp.logical_and(cand, jnp.logical_not(sel)), MASK_BIAS, 0.0)
        m = None
        s_blocks = []
        s_t = lax.dot_general(kh_ref[h, 0:(n + 1) * MOBA_BLOCK, :], q2b, _NT,
                              preferred_element_type=F32)
        for j in range(n + 1):
            s = s_t[blk(j)]
            if j == n:
                s = jnp.where(causal, s, -jnp.inf)
            mj = jnp.max(s.reshape(MOBA_BLOCK // 8, 8, MOBA_BLOCK), axis=0)
            if bias is not None and j < n:
                mj = mj + bias[j:j + 1]
            m = mj if m is None else jnp.maximum(m, mj)
            s_blocks.append(s)
            yield
        m = jnp.max(m, axis=0, keepdims=True)
        st["s"] = s_blocks
        st["shift"] = [m if (bias is None or j == n) else m - bias[j:j + 1]
                       for j in range(n + 1)]

    def output_pass(n, h, st):
        o_t = None
        for j in range(n + 1):
            p = jnp.exp2(st["s"][j] - st["shift"][j]).astype(BF16)
            oj = jnp.dot(vtb_ref[h, :, blk(j)], p, preferred_element_type=F32)
            o_t = oj if o_t is None else o_t + oj
            yield
        o_ref[h * HEAD_DIM:(h + 1) * HEAD_DIM, blk(n)] = (
            o_t[0:HEAD_DIM] / o_t[HEAD_DIM:HEAD_DIM + 1])

    units = [(n, h) for n in range(n_blocks) for h in range(2)]
    states = [dict() for _ in units]

    def all_scores():
        for u, unit in enumerate(units):
            yield from scores_pass(*unit, states[u])
            states[u]["ready"] = True
            yield

    def all_outputs():
        for u, unit in enumerate(units):
            ahead = min(u + ATTN_LEAD_UNITS - 1, len(units) - 1)
            while "ready" not in states[ahead]:
                yield
            yield from output_pass(*unit, states[u])

    _round_robin([all_scores(), all_outputs()])


def _attn_prompt(q, k, vt_all, layer, seq, name, cast=()):
    n_pairs, n, _ = q.shape
    batch = n // seq
    row_spec = pl.BlockSpec((None, seq, 128), lambda b, hp: (hp, b, 0))
    c_in, c_args, c_out, c_shapes = _cast_jobs(cast, batch * n_pairs,
                                               lambda b, hp: b * n_pairs + hp)
    return pl.pallas_call(
        _with_cast_jobs(_attn_prompt_kernel, 3, 1, len(cast)),
        grid=(batch, n_pairs),
        in_specs=[row_spec, row_spec,
                  pl.BlockSpec((None, None, 128, seq), lambda b, hp: (layer, b, hp, 0))] + c_in,
        out_specs=[pl.BlockSpec((None, 128, seq), lambda b, hp: (b, hp, 0))] + c_out,
        out_shape=[jax.ShapeDtypeStruct((batch, ATTN_WIDTH, seq), F32)] + c_shapes,
        scratch_shapes=[pltpu.VMEM((2, seq, 128), BF16), pltpu.VMEM((2, V_ROWS, seq), BF16)],
        compiler_params=pltpu.CompilerParams(
            dimension_semantics=("arbitrary", "arbitrary"), vmem_limit_bytes=VMEM_LIMIT),
        name=name,
    )(q, k, vt_all, *c_args)


def _sample_attn_steps(q_ref, kn_ref, vn_ref, o_ref, row0, n_new, k_pages, v_pages):
    n_pages = len(k_pages)
    page = k_pages[0].shape[1]
    pages_per_block = MOBA_BLOCK // page
    n_blocks = n_pages // pages_per_block
    n_rows = n_new * N_HEADS

    lane = lax.broadcasted_iota(jnp.int32, (N_HEADS, ATTN_WIDTH), 1)
    head = lax.broadcasted_iota(jnp.int32, (N_HEADS, ATTN_WIDTH), 0)
    head_lanes = (lane // HEAD_DIM == head).astype(F32)
    row_t = lax.broadcasted_iota(jnp.int32, (n_rows, 1), 0) // N_HEADS
    lane_w = lax.broadcasted_iota(jnp.int32, (ATTN_WIDTH, 128), 1)
    lane_g = lax.broadcasted_iota(jnp.int32, (n_rows, 128), 1)
    lane_gf = lane_g.astype(F32)

    def block_of(pages, j):
        return jnp.concatenate([pages[pages_per_block * j + i][...]
                                for i in range(pages_per_block)], axis=1)

    q_rows = [q_ref[row0 + t:row0 + t + 1, :] * head_lanes for t in range(n_new)]
    qbd = jnp.concatenate(q_rows, axis=0)
    qbd_b = qbd.astype(BF16)
    gate_w = jnp.zeros((ATTN_WIDTH, 128), F32)
    scores = []
    for j in range(n_blocks):
        kj = block_of(k_pages, j)
        mean_j = jnp.sum(kj, axis=1, keepdims=True) * (1.0 / MOBA_BLOCK)
        gate_w = jnp.where(lane_w == j, mean_j, gate_w)
        scores.append(jnp.dot(qbd_b, kj.astype(BF16), preferred_element_type=F32))
        yield

    gate = jnp.dot(qbd, gate_w, precision=lax.Precision.HIGHEST,
                   preferred_element_type=F32)
    sel = _top3_mask(gate, lane_g < n_blocks, lane_gf).astype(F32)
    yield

    m = jnp.full((n_rows, 1), -jnp.inf, F32)
    for j in range(n_blocks):
        sel_j = jnp.sum(jnp.where(lane_g == j, sel, 0.0), axis=-1, keepdims=True) > 0.5
        scores[j] = jnp.where(sel_j, scores[j], -jnp.inf)
        m = jnp.maximum(m, jnp.max(scores[j], axis=-1, keepdims=True))
    s_new = []
    for t in range(n_new):
        sc = jnp.sum(qbd * kn_ref[row0 + t:row0 + t + 1, :], axis=-1, keepdims=True)
        sc = jnp.where(row_t >= t, sc, -jnp.inf)
        s_new.append(sc)
        m = jnp.maximum(m, sc)
    yield

    l = jnp.zeros((n_rows, 1), F32)
    acc = jnp.zeros((n_rows, ATTN_WIDTH), F32)
    for j in range(n_blocks):
        p = jnp.exp2(scores[j] - m)
        l = l + jnp.sum(p, axis=-1, keepdims=True)
        acc = acc + lax.dot_general(p.astype(BF16), block_of(v_pages, j).astype(BF16), _NT,
                                    preferred_element_type=F32)
        yield
    for t in range(n_new):
        p = jnp.exp2(s_new[t] - m)
        l = l + p
        acc = acc + p * vn_ref[row0 + t:row0 + t + 1, :]

    acc = acc / l * jnp.concatenate([head_lanes] * n_new, axis=0)
    o_ref[row0:row0 + n_new, :] = jnp.sum(acc.reshape(n_new, N_HEADS, ATTN_WIDTH), axis=1)


def _attn_sample_kernel(pt_ref, q_ref, kn_ref, vn_ref, *refs, n_pages, group):
    del pt_ref
    o_ref = refs[2 * group * n_pages]
    n_new = q_ref.shape[0] // group
    _round_robin(
        _sample_attn_steps(q_ref, kn_ref, vn_ref, o_ref, s * n_new, n_new,
                           refs[s * n_pages:(s + 1) * n_pages],
                           refs[(group + s) * n_pages:(group + s + 1) * n_pages])
        for s in range(group))


def _attn_sample(q, k_new, v_new, n_new, cache_k, cache_v, page_table, layer, name):
    n_seq = q.shape[0] // n_new
    n_pages = page_table.shape[1]
    page = cache_k.shape[3]
    group = SAMPLE_GROUP
    new_spec = pl.BlockSpec((group * n_new, ATTN_WIDTH), lambda a, pt: (a, 0))

    def page_spec(s, j):
        return pl.BlockSpec((None, None, ATTN_WIDTH, page),
                            lambda a, pt: (layer, pt[a * group + s, j], 0, 0))

    page_specs = [page_spec(s, j) for s in range(group) for j in range(n_pages)]
    grid_spec = pltpu.PrefetchScalarGridSpec(
        num_scalar_prefetch=1,
        grid=(n_seq // group,),
        in_specs=[new_spec] * 3 + page_specs * 2,
        out_specs=new_spec,
    )
    n_page_args = group * n_pages
    return pl.pallas_call(
        functools.partial(_attn_sample_kernel, n_pages=n_pages, group=group),
        grid_spec=grid_spec,
        out_shape=jax.ShapeDtypeStruct((n_seq * n_new, ATTN_WIDTH), F32),
        compiler_params=pltpu.CompilerParams(
            dimension_semantics=("arbitrary",), vmem_limit_bytes=VMEM_LIMIT),
        name=name,
    )(page_table, q, k_new, v_new, *([cache_k] * n_page_args), *([cache_v] * n_page_args))


def _pool_windows(read, tm, count):
    groups = []
    for g, w in enumerate(POOL_WINDOWS):
        cur = read(0, g)
        total = cur
        for back in range(1, w):
            total = total + read(back, g)
        groups.append(total / count(w) - cur)
    return groups


def _mix_residual(groups, attn, y, wpool_ref, pscale_ref, wo_ref):
    pooled = [jnp.dot(groups[g].astype(BF16), wpool_ref[g], preferred_element_type=F32)
              for g in range(len(POOL_WINDOWS))]
    pooled = jnp.concatenate(pooled, axis=1) * pscale_ref[...]
    mix = jnp.concatenate([attn, pooled], axis=1).astype(BF16)
    return y + jnp.dot(mix, wo_ref[...], preferred_element_type=F32)


def _mix_sample_kernel(*refs):
    n_groups = len(POOL_WINDOWS)
    attn_ref = refs[0]
    p_refs = refs[1:1 + n_groups]
    (hist_ref, y_ref, wpool_ref, pscale_ref, wo_ref, gffn_ref, wup_ref, wdn_ref,
     o_ref, pooled_ref) = refs[1 + n_groups:]
    tm = y_ref.shape[0]
    n_seq = hist_ref.shape[1]
    n_new = tm // n_seq
    for t in range(n_new):
        def read(back, g, t=t):
            if back <= t:
                return p_refs[g][pl.ds(t - back, n_seq, stride=n_new), :]
            return hist_ref[POOL_HIST + t - back, :, g * POOL_GROUP:(g + 1) * POOL_GROUP]
        groups = _pool_windows(read, n_seq, lambda w: float(w))
        for g in range(n_groups):
            pooled_ref[g, pl.ds(t, n_seq, stride=n_new), :] = groups[g]
    groups = [pooled_ref[g] for g in range(n_groups)]
    y = _mix_residual(groups, attn_ref[...], y_ref[...], wpool_ref, pscale_ref, wo_ref)
    o_ref[...] = _ffn(y, gffn_ref, wup_ref, wdn_ref)


def _mix_prompt_kernel(attn_ref, p_ref, halo_ref, y_ref, wpool_ref, pscale_ref, wo_ref,
                       gffn_ref, wup_ref, wdn_ref, o_ref, xp_ref, *, seq_tiles):
    tm = p_ref.shape[0]
    seq_pos0 = (pl.program_id(0) % seq_tiles) * tm
    xp_ref[0:HALO, :] = jnp.where(seq_pos0 == 0, 0.0, halo_ref[...])
    xp_ref[HALO:, :] = p_ref[...]
    pos1 = (seq_pos0 + lax.broadcasted_iota(jnp.int32, (tm, 1), 0) + 1).astype(F32)

    def read(back, g):
        return xp_ref[pl.ds(HALO - back, tm), g * POOL_GROUP:(g + 1) * POOL_GROUP]
    groups = _pool_windows(read, tm, lambda w: jnp.minimum(pos1, float(w)))
    y = _mix_residual(groups, attn_ref[...].T, y_ref[...], wpool_ref, pscale_ref, wo_ref)
    o_ref[...] = _ffn(y, gffn_ref, wup_ref, wdn_ref)


def _mix_sample(attn, p, hist, y, wpool, pscale, wo, gffn, wup, wdn, name):
    tm = y.shape[0]
    row = lambda i: (0, 0)
    n_groups = len(POOL_WINDOWS)
    params = [wpool, pscale, wo, gffn, wup, wdn]
    group_view = lambda g: pl.BlockSpec((tm, POOL_GROUP), lambda i: (0, g))
    return pl.pallas_call(
        _mix_sample_kernel,
        grid=(1,),
        in_specs=([pl.BlockSpec((tm, ATTN_WIDTH), row)]
                  + [group_view(g) for g in range(n_groups)]
                  + [_const_spec(hist.shape), pl.BlockSpec((tm, D_MODEL), row)]
                  + [_param_spec(a) for a in params]),
        out_specs=pl.BlockSpec((tm, D_MODEL), row),
        out_shape=jax.ShapeDtypeStruct((tm, D_MODEL), F32),
        scratch_shapes=[pltpu.VMEM((n_groups, tm, POOL_GROUP), F32)],
        compiler_params=pltpu.CompilerParams(
            dimension_semantics=("arbitrary",), vmem_limit_bytes=VMEM_LIMIT),
        name=name,
    )(attn, *([p] * n_groups), hist, y, *[_param_arg(a) for a in params])


def _mix_prompt(attn_t, p, y, wpool, pscale, wo, gffn, wup, wdn, seq, name):
    n = y.shape[0]
    tm = MIX_TILE
    assert seq % tm == 0
    seq_tiles = seq // tm
    blocks_per_tile = tm // HALO
    params = [wpool, pscale, wo, gffn, wup, wdn]
    row = lambda i: (i, 0)
    return pl.pallas_call(
        functools.partial(_mix_prompt_kernel, seq_tiles=seq_tiles),
        grid=(n // tm,),
        in_specs=[
            pl.BlockSpec((None, ATTN_WIDTH, tm), lambda i: (i // seq_tiles, 0, i % seq_tiles)),
            pl.BlockSpec((tm, POOL_WIDTH), row),
            pl.BlockSpec((HALO, POOL_WIDTH),
                         lambda i: (jnp.maximum(i * blocks_per_tile - 1, 0), 0)),
            pl.BlockSpec((tm, D_MODEL), row),
        ] + [_param_spec(a) for a in params],
        out_specs=pl.BlockSpec((tm, D_MODEL), row),
        out_shape=jax.ShapeDtypeStruct((n, D_MODEL), F32),
        scratch_shapes=[pltpu.VMEM((tm + HALO, POOL_WIDTH), F32)],
        compiler_params=pltpu.CompilerParams(
            dimension_semantics=("arbitrary",), vmem_limit_bytes=VMEM_LIMIT),
        name=name,
    )(attn_t, p, p, y, *[_param_arg(a) for a in params])


def _odd_kernel(*refs, emit_v, final_norm):
    (x_ref, gmix_ref, win_ref, lng_ref, lnb_ref, ws_ref, bs_ref, wo_ref,
     gffn_ref, wup_ref, wdn_ref) = refs[:11]
    rest = list(refs[11:])
    gfin_ref = rest.pop(0) if final_norm else None
    o_ref = rest.pop(0)
    v_ref = rest.pop(0) if emit_v else None
    tm = x_ref.shape[0]

    x = x_ref[...]
    xn = _rms(x, gmix_ref[...]).astype(BF16)
    z = jax.nn.gelu(jnp.dot(xn, win_ref[...], preferred_element_type=F32))
    u = z[:, :D_MODEL]
    v = z[:, D_MODEL:]
    mu = jnp.mean(v, axis=-1, keepdims=True)
    vc = v - mu
    var = jnp.mean(vc * vc, axis=-1, keepdims=True)
    v = vc * lax.rsqrt(var + LN_EPS) * lng_ref[...] + lnb_ref[...]
    if emit_v:
        v_ref[...] = v
    vb = v.astype(BF16)
    bias = bs_ref[...]
    chunks = []
    for c in range(tm // SGU_CHUNK):
        rows = slice(c * SGU_CHUNK, (c + 1) * SGU_CHUNK)
        cols = [jnp.dot(ws_ref[g], vb[rows, g * 128:(g + 1) * 128],
                        preferred_element_type=F32) for g in range(N_SGU_GROUPS)]
        chunks.append(jnp.concatenate(cols, axis=1) + bias)
    gated = (u * jnp.concatenate(chunks, axis=0)).astype(BF16)
    y = x + jnp.dot(gated, wo_ref[...], preferred_element_type=F32)
    y = _ffn(y, gffn_ref, wup_ref, wdn_ref)
    if final_norm:
        y = _rms(y, gfin_ref[...])
    o_ref[...] = y


def _odd_layer(x, gmix, win, lng, lnb, ws, bs, wo, gffn, wup, wdn, gfin, emit_v, name):
    n = x.shape[0]
    tm = min(ROW_TILE, n)
    row = lambda i: (i, 0)
    final_norm = gfin is not None
    params = [gmix, win, lng, lnb, ws, bs, wo, gffn, wup, wdn]
    if final_norm:
        params.append(gfin)
    args = [x] + [_param_arg(a) for a in params]
    in_specs = [pl.BlockSpec((tm, D_MODEL), row)] + [_param_spec(a) for a in params]
    out_specs = [pl.BlockSpec((tm, D_MODEL), row)]
    out_shape = [jax.ShapeDtypeStruct((n, D_MODEL), F32)]
    if emit_v:
        out_specs.append(pl.BlockSpec((tm, D_MODEL), row))
        out_shape.append(jax.ShapeDtypeStruct((n, D_MODEL), F32))
    outs = pl.pallas_call(
        functools.partial(_odd_kernel, emit_v=emit_v, final_norm=final_norm),
        grid=(n // tm,),
        in_specs=in_specs,
        out_specs=out_specs,
        out_shape=out_shape,
        compiler_params=pltpu.CompilerParams(
            dimension_semantics=("arbitrary",), vmem_limit_bytes=VMEM_LIMIT),
        name=name,
    )(*args)
    return outs if emit_v else (outs[0], None)


def _rope_tables(pos):
    inv = ROPE_THETA ** (-jnp.arange(0, HEAD_DIM, 2, dtype=F32) / HEAD_DIM)
    ang = pos.astype(F32)[:, None] * inv[None, :]
    cos, sin = jnp.cos(ang), jnp.sin(ang)
    return (jnp.concatenate([cos, cos, cos, cos], axis=1),
            jnp.concatenate([-sin, sin, -sin, sin], axis=1))


def _sgu_weights(w_s, b_s, length):
    tril = jnp.tril(jnp.ones((length, length), dtype=bool))
    w = jnp.where(tril, w_s[:, :length, :length], 0.0)
    reps = SGU_CHUNK // length
    if reps > 1:
        eye = jnp.eye(reps, dtype=w.dtype)
        w = jnp.einsum('ab,gts->gatbs', eye, w).reshape(
            N_SGU_GROUPS, SGU_CHUNK, SGU_CHUNK)
    b = jnp.tile(b_s[:, :length], (1, reps))
    b = jnp.repeat(b.T, D_MODEL // N_SGU_GROUPS, axis=1)
    return w.astype(BF16), b


def kernel(x_prompt, x_sample, cache_k, cache_v, state_pool, page_table, norm_mix, w_in_even, w_o_even, w_pool, pool_scale, w_in_odd, sgu_ln_g, sgu_ln_b, sgu_w, sgu_b, w_o_odd, norm_ffn, w_ffn_up, w_ffn_down, norm_final):
    batch, seq, _ = x_prompt.shape
    n_seq, n_new, _ = x_sample.shape
    depth = norm_mix.shape[0]
    n_phys, page = cache_k.shape[1], cache_k.shape[2]
    past_len = page_table.shape[1] * page

    yp = x_prompt.reshape(batch * seq, D_MODEL)
    ys = x_sample.reshape(n_seq * n_new, D_MODEL)
    ck = jnp.transpose(cache_k, (0, 1, 3, 4, 2)).reshape(cache_k.shape[0], n_phys, ATTN_WIDTH, page)
    cv = jnp.transpose(cache_v, (0, 1, 3, 4, 2)).reshape(cache_v.shape[0], n_phys, ATTN_WIDTH, page)

    cos_p, sin_p = _rope_tables(jnp.arange(seq))
    cos_s, sin_s = _rope_tables(past_len + jnp.arange(n_seq * n_new) % n_new)

    w_pool_b = w_pool.astype(BF16)
    w_bf16 = {("in_even", 0): w_in_even[0].astype(BF16)}
    sample_k, sample_v = [], []
    prompt_pool, sample_pool, sample_sgu = [], [], []
    n_even = (depth + 1) // 2
    kv_t = None
    for l in range(depth):
        gmix = (norm_mix[:, None, :], l)
        gffn = (norm_ffn[:, None, :], l)
        if l % 2 == 0:
            i = l // 2
            pscale = (pool_scale[:, None, :], i)
            w_in_l, w_pool_l = w_bf16["in_even", i], (w_pool_b, i)
            q, k, kt_all, vt_all, p = _inproj_even(
                yp, gmix, w_in_l, cos_p, sin_p, f"inproj_p{l}",
                kv_t_shape=(n_even, batch, ATTN_WIDTH, seq), kv_t=kv_t, layer=i)
            kv_t = (kt_all, vt_all)
            jobs = {("up", l): (w_ffn_up, l), ("dn", l): (w_ffn_down, l),
                    ("o_even", i): (w_o_even, i)}
            if l + 1 < depth:
                jobs.update({("up", l + 1): (w_ffn_up, l + 1), ("dn", l + 1): (w_ffn_down, l + 1),
                             ("in_odd", i): (w_in_odd, i), ("o_odd", i): (w_o_odd, i)})
            if i + 1 < n_even:
                jobs["in_even", i + 1] = (w_in_even, i + 1)
            attn, *cast = _attn_prompt(q, k, vt_all, i, seq, f"attn_p{l}",
                                       cast=list(jobs.values()))
            w_bf16.update(zip(jobs.keys(), cast))
            w_o_l, w_up_l, w_dn_l = w_bf16["o_even", i], w_bf16["up", l], w_bf16["dn", l]
            yp = _mix_prompt(attn, p, yp, w_pool_l, pscale, w_o_l, gffn, w_up_l, w_dn_l,
                             seq, f"mix_p{l}")
            prompt_pool.append(p.reshape(batch, seq, POOL_WIDTH)[:, seq - POOL_HIST:, :])

            q, k, v, p = _inproj_even(ys, gmix, w_in_l, cos_s, sin_s, f"inproj_s{l}")
            attn = _attn_sample(q, k, v, n_new, ck, cv, page_table, i, f"attn_s{l}")
            hist = jnp.transpose(state_pool[i], (1, 0, 2))
            ys = _mix_sample(attn, p, hist, ys, w_pool_l,
                             pscale, w_o_l, gffn, w_up_l, w_dn_l, f"mix_s{l}")
            sample_k.append(k.reshape(n_seq, n_new, N_HEADS, HEAD_DIM))
            sample_v.append(v.reshape(n_seq, n_new, N_HEADS, HEAD_DIM))
            sample_pool.append(jnp.concatenate(
                [state_pool[i], p.reshape(n_seq, n_new, POOL_WIDTH)], axis=1)[:, -POOL_HIST:, :])
        else:
            j = l // 2
            gfin = norm_final[None, :] if l == depth - 1 else None
            lng = (sgu_ln_g[:, None, :], j)
            lnb = (sgu_ln_b[:, None, :], j)
            w_in_l, w_o_l = w_bf16["in_odd", j], w_bf16["o_odd", j]
            ws_p, bs_p = _sgu_weights(sgu_w[j], sgu_b[j], SGU_CHUNK)
            ws_s, bs_s = _sgu_weights(sgu_w[j], sgu_b[j], n_new)
            w_up_l, w_dn_l = w_bf16["up", l], w_bf16["dn", l]
            yp, _ = _odd_layer(yp, gmix, w_in_l, lng, lnb, ws_p, bs_p, w_o_l,
                               gffn, w_up_l, w_dn_l, gfin, False, f"odd_p{l}")
            ys, v_rows = _odd_layer(ys, gmix, w_in_l, lng, lnb, ws_s, bs_s, w_o_l,
                                    gffn, w_up_l, w_dn_l, gfin, True, f"odd_s{l}")
            sample_sgu.append(v_rows.reshape(n_seq, n_new, D_MODEL))
    if depth % 2 == 1:
        raise NotImplementedError("the final norm is fused into an odd last layer")
    prompt_k, prompt_v = [
        jnp.transpose(t.reshape(n_even, batch, N_HEADS, HEAD_DIM, seq), (0, 1, 4, 2, 3))
        for t in kv_t]
    return (yp.reshape(batch, seq, D_MODEL), ys.reshape(n_seq, n_new, D_MODEL),
            prompt_k, prompt_v,
            jnp.stack(sample_k), jnp.stack(sample_v),
            jnp.stack(prompt_pool), jnp.stack(sample_pool),
            jnp.stack(sample_sgu))
```

```python
import functools

import jax
import jax.numpy as jnp
from jax import lax
from jax.experimental import pallas as pl
from jax.experimental.pallas import tpu as pltpu

F32 = jnp.float32
BF16 = jnp.bfloat16

D_MODEL = 1024
HEAD_DIM = 64
N_HEADS = 8
ATTN_WIDTH = 512
POOL_WIDTH = 512
POOL_WINDOWS = (2, 4, 8, 16)
POOL_GROUP = 128
POOL_HIST = 15
MOBA_BLOCK = 256
MOBA_TOPK = 3
ROPE_THETA = 10000.0
SGU_CHUNK = 128
N_SGU_GROUPS = 8
D_FF = 4096
NORM_EPS = 1e-6
LN_EPS = 1e-5

ROW_TILE = 1024
MIX_TILE = 1024
FF_CHUNK = 1024
MIX_FF_CHUNK = 512
HALO = 16
MASK_BIAS = -1e30
Q_SCALE = HEAD_DIM ** -0.5 * 1.4426950408889634
V_ROWS = HEAD_DIM + 16
SAMPLE_GROUP = 2
ATTN_LEAD_UNITS = 2
VMEM_LIMIT = 56 * 1024 * 1024

_NT = (((1,), (1,)), ((), ()))


def _rms(x, g):
    ms = jnp.mean(x * x, axis=-1, keepdims=True)
    return x * lax.rsqrt(ms + NORM_EPS) * g


def _ffn(y, g_ref, wup_ref, wdn_ref, chunk=FF_CHUNK):
    hn = _rms(y, g_ref[...]).astype(BF16)
    acc = y
    for c in range(D_FF // chunk):
        sl = slice(c * chunk, (c + 1) * chunk)
        h = jnp.dot(hn, wup_ref[:, sl], preferred_element_type=F32)
        h = jnp.maximum(h, 0.0)
        h = (h * h).astype(BF16)
        acc = acc + jnp.dot(h, wdn_ref[sl, :], preferred_element_type=F32)
    return acc


def _round_robin(gens):
    gens = list(gens)
    while gens:
        for g in list(gens):
            if next(g, StopIteration) is StopIteration:
                gens.remove(g)


def _const_spec(shape):
    nd = len(shape)
    return pl.BlockSpec(shape, lambda *_: (0,) * nd, pipeline_mode=pl.Buffered(1))


def _param_spec(a):
    if isinstance(a, tuple):
        w, layer = a
        nd = w.ndim - 1
        return pl.BlockSpec((None,) + w.shape[1:], lambda *_: (layer,) + (0,) * nd,
                            pipeline_mode=pl.Buffered(1))
    return _const_spec(a.shape)


def _param_arg(a):
    return a[0] if isinstance(a, tuple) else a


def _cast_jobs(jobs, n_steps, flat_step):
    in_specs, args, out_specs, out_shapes = [], [], [], []
    for w, layer in jobs:
        rows, cols = w.shape[1:]
        chunk = rows // n_steps
        assert chunk * n_steps == rows and chunk % 16 == 0
        in_specs.append(pl.BlockSpec((None, chunk, cols),
                                     lambda *idx, layer=layer: (layer, flat_step(*idx), 0)))
        out_specs.append(pl.BlockSpec((chunk, cols), lambda *idx: (flat_step(*idx), 0)))
        args.append(w)
        out_shapes.append(jax.ShapeDtypeStruct((rows, cols), BF16))
    return in_specs, args, out_specs, out_shapes


def _with_cast_jobs(kernel_fn, n_in, n_out, n_jobs):
    if not n_jobs:
        return kernel_fn

    def wrapped(*refs):
        a, b, c = n_in + n_jobs, n_in + n_jobs + n_out, n_in + 2 * n_jobs + n_out
        for src, dst in zip(refs[n_in:a], refs[b:c]):
            dst[...] = src[...].astype(BF16)
        kernel_fn(*refs[:n_in], *refs[a:b], *refs[c:])
    return wrapped


def _inproj_even_kernel(x_ref, g_ref, w_ref, cos_ref, sin_ref, *refs, transposed):
    if transposed:
        q_ref, k_ref, kt_ref, vt_ref, p_ref = refs[-5:]
    else:
        q_ref, k_ref, v_ref, p_ref = refs
    xn = _rms(x_ref[...], g_ref[...]).astype(BF16)
    proj = jnp.dot(xn, w_ref[...], preferred_element_type=F32)
    cos = jnp.concatenate([cos_ref[...]] * 4, axis=1)
    sin = jnp.concatenate([sin_ref[...]] * 4, axis=1)
    lane = lax.broadcasted_iota(jnp.int32, cos.shape, 1) % HEAD_DIM
    first_half = lane < HEAD_DIM // 2

    def rot(a):
        partner = jnp.where(first_half,
                            pltpu.roll(a, ATTN_WIDTH - HEAD_DIM // 2, 1),
                            pltpu.roll(a, HEAD_DIM // 2, 1))
        return a * cos + partner * sin

    q_ref[...] = rot(proj[:, 0:ATTN_WIDTH]) * Q_SCALE
    k = rot(proj[:, ATTN_WIDTH:2 * ATTN_WIDTH])
    v = proj[:, 2 * ATTN_WIDTH:3 * ATTN_WIDTH]
    k_ref[...] = k
    if transposed:
        kt_ref[...] = k.T
        vt_ref[...] = v.T
    else:
        v_ref[...] = v
    p_ref[...] = proj[:, 3 * ATTN_WIDTH:]


def _inproj_even(x, g, w, cos, sin, name, kv_t_shape=None, kv_t=None, layer=0):
    n = x.shape[0]
    tm = min(ROW_TILE, n)
    n_pos_blocks = cos.shape[0] // tm
    row = lambda i: (i, 0)
    out = jax.ShapeDtypeStruct((n, ATTN_WIDTH), F32)
    row_spec = pl.BlockSpec((tm, ATTN_WIDTH), row)
    in_specs = [
        pl.BlockSpec((tm, D_MODEL), row),
        _param_spec(g),
        _param_spec(w),
        pl.BlockSpec((tm, 128), lambda i: (i % n_pos_blocks, 0)),
        pl.BlockSpec((tm, 128), lambda i: (i % n_pos_blocks, 0)),
    ]
    args = [x, _param_arg(g), _param_arg(w), cos, sin]
    aliases = {}
    if kv_t_shape is None:
        out_specs, out_shape = [row_spec] * 4, [out] * 4
    else:
        seq_tiles = kv_t_shape[3] // tm
        t_spec = pl.BlockSpec((None, None, ATTN_WIDTH, tm),
                              lambda i: (layer, i // seq_tiles, 0, i % seq_tiles))
        t_shape = jax.ShapeDtypeStruct(kv_t_shape, F32)
        if kv_t is not None:
            aliases = {len(args): 2, len(args) + 1: 3}
            in_specs += [pl.BlockSpec(memory_space=pl.ANY)] * 2
            args += list(kv_t)
        out_specs = [row_spec, row_spec, t_spec, t_spec, row_spec]
        out_shape = [out, out, t_shape, t_shape, out]
    return pl.pallas_call(
        functools.partial(_inproj_even_kernel, transposed=kv_t_shape is not None),
        grid=(n // tm,),
        in_specs=in_specs,
        out_specs=out_specs,
        out_shape=out_shape,
        input_output_aliases=aliases,
        compiler_params=pltpu.CompilerParams(
            dimension_semantics=("arbitrary",), vmem_limit_bytes=VMEM_LIMIT),
        name=name,
    )(*args)


def _top3_mask(gate, cand, index_f, axis=-1):
    g = jnp.where(cand, gate, -jnp.inf)
    sel = jnp.zeros(gate.shape, jnp.bool_)
    for _ in range(MOBA_TOPK):
        m = jnp.max(g, axis=axis, keepdims=True)
        idx = jnp.min(jnp.where(g == m, index_f, 1e9), axis=axis, keepdims=True)
        pick = index_f == idx
        sel = jnp.logical_or(sel, pick)
        g = jnp.where(pick, -jnp.inf, g)
    return sel


def _attn_prompt_kernel(q_ref, k_ref, vt_ref, o_ref, kh_ref, vtb_ref):
    seq = k_ref.shape[0]
    n_blocks = seq // MOBA_BLOCK
    lane1 = lax.broadcasted_iota(jnp.int32, (1, 128), 1)
    lane_s = lax.broadcasted_iota(jnp.int32, (seq, 128), 1)
    k2 = k_ref[...]
    kmean = jnp.sum(k2.reshape(n_blocks, MOBA_BLOCK, 128), axis=1) * (1.0 / MOBA_BLOCK)
    gate_w = []
    for h in range(2):
        vtb_ref[h, 0:HEAD_DIM, :] = vt_ref[h * HEAD_DIM:(h + 1) * HEAD_DIM, :].astype(BF16)
        vtb_ref[h, HEAD_DIM:, :] = jnp.ones((V_ROWS - HEAD_DIM, seq), BF16)
        own = jnp.logical_and(lane_s >= h * HEAD_DIM, lane_s < (h + 1) * HEAD_DIM)
        kh_ref[h] = jnp.where(own, k2, 0.0).astype(BF16)
        own1 = jnp.logical_and(lane1 >= h * HEAD_DIM, lane1 < (h + 1) * HEAD_DIM)
        gate_w.append(jnp.where(own1, kmean, 0.0))

    blk_i = lax.broadcasted_iota(jnp.int32, (n_blocks, MOBA_BLOCK), 0)
    blk_f = blk_i.astype(F32)
    key_i = lax.broadcasted_iota(jnp.int32, (MOBA_BLOCK, MOBA_BLOCK), 0)
    qry_i = lax.broadcasted_iota(jnp.int32, (MOBA_BLOCK, MOBA_BLOCK), 1)
    causal = key_i <= qry_i

    def blk(j):
        return slice(j * MOBA_BLOCK, (j + 1) * MOBA_BLOCK)

    def scores_pass(n, h, st):
        q2 = q_ref[blk(n), :]
        q2b = q2.astype(BF16)
        bias = None
        if n > MOBA_TOPK:
            gate = lax.dot_general(gate_w[h], q2, _NT, precision=lax.Precision.HIGHEST,
                                   preferred_element_type=F32)
            cand = blk_i < n
            sel = _top3_mask(gate, cand, blk_f, axis=0)
            bias = jnp.where(jnp.logical_and(cand, jnp.logical_not(sel)), MASK_BIAS, 0.0)
        m = None
        s_blocks = []
        s_t = lax.dot_general(kh_ref[h, 0:(n + 1) * MOBA_BLOCK, :], q2b, _NT,
                              preferred_element_type=F32)
        for j in range(n + 1):
            s = s_t[blk(j)]
            if j == n:
                s = jnp.where(causal, s, -jnp.inf)
            mj = jnp.max(s.reshape(MOBA_BLOCK // 8, 8, MOBA_BLOCK), axis=0)
            if bias is not None and j < n:
                mj = mj + bias[j:j + 1]
            m = mj if m is None else jnp.maximum(m, mj)
            s_blocks.append(s)
            yield
        m = jnp.max(m, axis=0, keepdims=True)
        st["s"] = s_blocks
        st["shift"] = [m if (bias is None or j == n) else m - bias[j:j + 1]
                       for j in range(n + 1)]

    def output_pass(n, h, st):
        o_t = None
        for j in range(n + 1):
            p = jnp.exp2(st["s"][j] - st["shift"][j]).astype(BF16)
            oj = jnp.dot(vtb_ref[h, :, blk(j)], p, preferred_element_type=F32)
            o_t = oj if o_t is None else o_t + oj
            yield
        o_ref[h * HEAD_DIM:(h + 1) * HEAD_DIM, blk(n)] = (
            o_t[0:HEAD_DIM] / o_t[HEAD_DIM:HEAD_DIM + 1])

    units = [(n, h) for n in range(n_blocks) for h in range(2)]
    states = [dict() for _ in units]

    def all_scores():
        for u, unit in enumerate(units):
            yield from scores_pass(*unit, states[u])
            states[u]["ready"] = True
            yield

    def all_outputs():
        for u, unit in enumerate(units):
            ahead = min(u + ATTN_LEAD_UNITS - 1, len(units) - 1)
            while "ready" not in states[ahead]:
                yield
            yield from output_pass(*unit, states[u])

    _round_robin([all_scores(), all_outputs()])


def _attn_prompt(q, k, vt_all, layer, seq, name, cast=()):
    n = q.shape[0]
    batch = n // seq
    n_pairs = ATTN_WIDTH // 128
    row_spec = pl.BlockSpec((seq, 128), lambda b, hp: (b, hp))
    c_in, c_args, c_out, c_shapes = _cast_jobs(cast, batch * n_pairs,
                                               lambda b, hp: b * n_pairs + hp)
    return pl.pallas_call(
        _with_cast_jobs(_attn_prompt_kernel, 3, 1, len(cast)),
        grid=(batch, n_pairs),
        in_specs=[row_spec, row_spec,
                  pl.BlockSpec((None, None, 128, seq), lambda b, hp: (layer, b, hp, 0))] + c_in,
        out_specs=[pl.BlockSpec((None, 128, seq), lambda b, hp: (b, hp, 0))] + c_out,
        out_shape=[jax.ShapeDtypeStruct((batch, ATTN_WIDTH, seq), F32)] + c_shapes,
        scratch_shapes=[pltpu.VMEM((2, seq, 128), BF16), pltpu.VMEM((2, V_ROWS, seq), BF16)],
        compiler_params=pltpu.CompilerParams(
            dimension_semantics=("arbitrary", "arbitrary"), vmem_limit_bytes=VMEM_LIMIT),
        name=name,
    )(q, k, vt_all, *c_args)


def _sample_attn_steps(q_ref, kn_ref, vn_ref, o_ref, row0, n_new, k_pages, v_pages):
    n_pages = len(k_pages)
    page = k_pages[0].shape[1]
    pages_per_block = MOBA_BLOCK // page
    n_blocks = n_pages // pages_per_block
    n_rows = n_new * N_HEADS

    lane = lax.broadcasted_iota(jnp.int32, (N_HEADS, ATTN_WIDTH), 1)
    head = lax.broadcasted_iota(jnp.int32, (N_HEADS, ATTN_WIDTH), 0)
    head_lanes = (lane // HEAD_DIM == head).astype(F32)
    row_t = lax.broadcasted_iota(jnp.int32, (n_rows, 1), 0) // N_HEADS
    lane_w = lax.broadcasted_iota(jnp.int32, (ATTN_WIDTH, 128), 1)
    lane_g = lax.broadcasted_iota(jnp.int32, (n_rows, 128), 1)
    lane_gf = lane_g.astype(F32)

    def block_of(pages, j):
        return jnp.concatenate([pages[pages_per_block * j + i][...]
                                for i in range(pages_per_block)], axis=1)

    q_rows = [q_ref[row0 + t:row0 + t + 1, :] * head_lanes for t in range(n_new)]
    qbd = jnp.concatenate(q_rows, axis=0)
    qbd_b = qbd.astype(BF16)
    gate_w = jnp.zeros((ATTN_WIDTH, 128), F32)
    scores = []
    for j in range(n_blocks):
        kj = block_of(k_pages, j)
        mean_j = jnp.sum(kj, axis=1, keepdims=True) * (1.0 / MOBA_BLOCK)
        gate_w = jnp.where(lane_w == j, mean_j, gate_w)
        scores.append(jnp.dot(qbd_b, kj.astype(BF16), preferred_element_type=F32))
        yield

    gate = jnp.dot(qbd, gate_w, precision=lax.Precision.HIGHEST,
                   preferred_element_type=F32)
    sel = _top3_mask(gate, lane_g < n_blocks, lane_gf).astype(F32)
    yield

    m = jnp.full((n_rows, 1), -jnp.inf, F32)
    for j in range(n_blocks):
        sel_j = jnp.sum(jnp.where(lane_g == j, sel, 0.0), axis=-1, keepdims=True) > 0.5
        scores[j] = jnp.where(sel_j, scores[j], -jnp.inf)
        m = jnp.maximum(m, jnp.max(scores[j], axis=-1, keepdims=True))
    s_new = []
    for t in range(n_new):
        sc = jnp.sum(qbd * kn_ref[row0 + t:row0 + t + 1, :], axis=-1, keepdims=True)
        sc = jnp.where(row_t >= t, sc, -jnp.inf)
        s_new.append(sc)
        m = jnp.maximum(m, sc)
    yield

    l = jnp.zeros((n_rows, 1), F32)
    acc = jnp.zeros((n_rows, ATTN_WIDTH), F32)
    for j in range(n_blocks):
        p = jnp.exp2(scores[j] - m)
        l = l + jnp.sum(p, axis=-1, keepdims=True)
        acc = acc + lax.dot_general(p.astype(BF16), block_of(v_pages, j).astype(BF16), _NT,
                                    preferred_element_type=F32)
        yield
    for t in range(n_new):
        p = jnp.exp2(s_new[t] - m)
        l = l + p
        acc = acc + p * vn_ref[row0 + t:row0 + t + 1, :]

    acc = acc / l * jnp.concatenate([head_lanes] * n_new, axis=0)
    o_ref[row0:row0 + n_new, :] = jnp.sum(acc.reshape(n_new, N_HEADS, ATTN_WIDTH), axis=1)


def _attn_sample_kernel(pt_ref, q_ref, kn_ref, vn_ref, *refs, n_pages, group):
    del pt_ref
    o_ref = refs[2 * group * n_pages]
    n_new = q_ref.shape[0] // group
    _round_robin(
        _sample_attn_steps(q_ref, kn_ref, vn_ref, o_ref, s * n_new, n_new,
                           refs[s * n_pages:(s + 1) * n_pages],
                           refs[(group + s) * n_pages:(group + s + 1) * n_pages])
        for s in range(group))


def _attn_sample(q, k_new, v_new, n_new, cache_k, cache_v, page_table, layer, name):
    n_seq = q.shape[0] // n_new
    n_pages = page_table.shape[1]
    page = cache_k.shape[3]
    group = SAMPLE_GROUP
    new_spec = pl.BlockSpec((group * n_new, ATTN_WIDTH), lambda a, pt: (a, 0))

    def page_spec(s, j):
        return pl.BlockSpec((None, None, ATTN_WIDTH, page),
                            lambda a, pt: (layer, pt[a * group + s, j], 0, 0))

    page_specs = [page_spec(s, j) for s in range(group) for j in range(n_pages)]
    grid_spec = pltpu.PrefetchScalarGridSpec(
        num_scalar_prefetch=1,
        grid=(n_seq // group,),
        in_specs=[new_spec] * 3 + page_specs * 2,
        out_specs=new_spec,
    )
    n_page_args = group * n_pages
    return pl.pallas_call(
        functools.partial(_attn_sample_kernel, n_pages=n_pages, group=group),
        grid_spec=grid_spec,
        out_shape=jax.ShapeDtypeStruct((n_seq * n_new, ATTN_WIDTH), F32),
        compiler_params=pltpu.CompilerParams(
            dimension_semantics=("arbitrary",), vmem_limit_bytes=VMEM_LIMIT),
        name=name,
    )(page_table, q, k_new, v_new, *([cache_k] * n_page_args), *([cache_v] * n_page_args))


def _pool_windows(read, tm, count):
    groups = []
    for g, w in enumerate(POOL_WINDOWS):
        cur = read(0, g)
        total = cur
        for back in range(1, w):
            total = total + read(back, g)
        groups.append(total / count(w) - cur)
    return groups


def _mix_residual(groups, attn, y, wpool_ref, pscale_ref, wo_ref):
    pooled = [jnp.dot(groups[g].astype(BF16), wpool_ref[g], preferred_element_type=F32)
              for g in range(len(POOL_WINDOWS))]
    pooled = jnp.concatenate(pooled, axis=1) * pscale_ref[...]
    mix = jnp.concatenate([attn, pooled], axis=1).astype(BF16)
    return y + jnp.dot(mix, wo_ref[...], preferred_element_type=F32)


def _mix_sample_kernel(*refs):
    n_groups = len(POOL_WINDOWS)
    attn_ref = refs[0]
    p_refs = refs[1:1 + n_groups]
    (hist_ref, y_ref, wpool_ref, pscale_ref, wo_ref, gffn_ref, wup_ref, wdn_ref,
     o_ref, pooled_ref) = refs[1 + n_groups:]
    tm = y_ref.shape[0]
    n_seq = hist_ref.shape[1]
    n_new = tm // n_seq
    for t in range(n_new):
        def read(back, g, t=t):
            if back <= t:
                return p_refs[g][pl.ds(t - back, n_seq, stride=n_new), :]
            return hist_ref[POOL_HIST + t - back, :, g * POOL_GROUP:(g + 1) * POOL_GROUP]
        groups = _pool_windows(read, n_seq, lambda w: float(w))
        for g in range(n_groups):
            pooled_ref[g, pl.ds(t, n_seq, stride=n_new), :] = groups[g]
    groups = [pooled_ref[g] for g in range(n_groups)]
    y = _mix_residual(groups, attn_ref[...], y_ref[...], wpool_ref, pscale_ref, wo_ref)
    o_ref[...] = _ffn(y, gffn_ref, wup_ref, wdn_ref)


def _mix_prompt_kernel(attn_ref, p_ref, halo_ref, y_ref, wpool_ref, pscale_ref, wo_ref,
                       gffn_ref, wup_ref, wdn_ref, o_ref, xp_ref, *, seq_tiles):
    tm = p_ref.shape[0]
    seq_pos0 = (pl.program_id(0) % seq_tiles) * tm
    xp_ref[0:HALO, :] = jnp.where(seq_pos0 == 0, 0.0, halo_ref[...])
    xp_ref[HALO:, :] = p_ref[...]
    pos1 = (seq_pos0 + lax.broadcasted_iota(jnp.int32, (tm, 1), 0) + 1).astype(F32)

    def read(back, g):
        return xp_ref[pl.ds(HALO - back, tm), g * POOL_GROUP:(g + 1) * POOL_GROUP]
    groups = _pool_windows(read, tm, lambda w: jnp.minimum(pos1, float(w)))
    y = _mix_residual(groups, attn_ref[...].T, y_ref[...], wpool_ref, pscale_ref, wo_ref)
    o_ref[...] = _ffn(y, gffn_ref, wup_ref, wdn_ref, chunk=MIX_FF_CHUNK)


def _mix_sample(attn, p, hist, y, wpool, pscale, wo, gffn, wup, wdn, name):
    tm = y.shape[0]
    row = lambda i: (0, 0)
    n_groups = len(POOL_WINDOWS)
    params = [wpool, pscale, wo, gffn, wup, wdn]
    group_view = lambda g: pl.BlockSpec((tm, POOL_GROUP), lambda i: (0, g))
    return pl.pallas_call(
        _mix_sample_kernel,
        grid=(1,),
        in_specs=([pl.BlockSpec((tm, ATTN_WIDTH), row)]
                  + [group_view(g) for g in range(n_groups)]
                  + [_const_spec(hist.shape), pl.BlockSpec((tm, D_MODEL), row)]
                  + [_param_spec(a) for a in params]),
        out_specs=pl.BlockSpec((tm, D_MODEL), row),
        out_shape=jax.ShapeDtypeStruct((tm, D_MODEL), F32),
        scratch_shapes=[pltpu.VMEM((n_groups, tm, POOL_GROUP), F32)],
        compiler_params=pltpu.CompilerParams(
            dimension_semantics=("arbitrary",), vmem_limit_bytes=VMEM_LIMIT),
        name=name,
    )(attn, *([p] * n_groups), hist, y, *[_param_arg(a) for a in params])


def _mix_prompt(attn_t, p, y, wpool, pscale, wo, gffn, wup, wdn, seq, name):
    n = y.shape[0]
    tm = MIX_TILE
    assert seq % tm == 0
    seq_tiles = seq // tm
    blocks_per_tile = tm // HALO
    params = [wpool, pscale, wo, gffn, wup, wdn]
    row = lambda i: (i, 0)
    return pl.pallas_call(
        functools.partial(_mix_prompt_kernel, seq_tiles=seq_tiles),
        grid=(n // tm,),
        in_specs=[
            pl.BlockSpec((None, ATTN_WIDTH, tm), lambda i: (i // seq_tiles, 0, i % seq_tiles)),
            pl.BlockSpec((tm, POOL_WIDTH), row),
            pl.BlockSpec((HALO, POOL_WIDTH),
                         lambda i: (jnp.maximum(i * blocks_per_tile - 1, 0), 0)),
            pl.BlockSpec((tm, D_MODEL), row),
        ] + [_param_spec(a) for a in params],
        out_specs=pl.BlockSpec((tm, D_MODEL), row),
        out_shape=jax.ShapeDtypeStruct((n, D_MODEL), F32),
        scratch_shapes=[pltpu.VMEM((tm + HALO, POOL_WIDTH), F32)],
        compiler_params=pltpu.CompilerParams(
            dimension_semantics=("arbitrary",), vmem_limit_bytes=VMEM_LIMIT),
        name=name,
    )(attn_t, p, p, y, *[_param_arg(a) for a in params])


def _odd_kernel(*refs, emit_v, final_norm):
    (x_ref, gmix_ref, win_ref, lng_ref, lnb_ref, ws_ref, bs_ref, wo_ref,
     gffn_ref, wup_ref, wdn_ref) = refs[:11]
    rest = list(refs[11:])
    gfin_ref = rest.pop(0) if final_norm else None
    o_ref = rest.pop(0)
    v_ref = rest.pop(0) if emit_v else None
    tm = x_ref.shape[0]

    x = x_ref[...]
    xn = _rms(x, gmix_ref[...]).astype(BF16)
    z = jax.nn.gelu(jnp.dot(xn, win_ref[...], preferred_element_type=F32))
    u = z[:, :D_MODEL]
    v = z[:, D_MODEL:]
    mu = jnp.mean(v, axis=-1, keepdims=True)
    vc = v - mu
    var = jnp.mean(vc * vc, axis=-1, keepdims=True)
    v = vc * lax.rsqrt(var + LN_EPS) * lng_ref[...] + lnb_ref[...]
    if emit_v:
        v_ref[...] = v
    vb = v.astype(BF16)
    bias = bs_ref[...]
    chunks = []
    for c in range(tm // SGU_CHUNK):
        rows = slice(c * SGU_CHUNK, (c + 1) * SGU_CHUNK)
        cols = [jnp.dot(ws_ref[g], vb[rows, g * 128:(g + 1) * 128],
                        preferred_element_type=F32) for g in range(N_SGU_GROUPS)]
        chunks.append(jnp.concatenate(cols, axis=1) + bias)
    gated = (u * jnp.concatenate(chunks, axis=0)).astype(BF16)
    y = x + jnp.dot(gated, wo_ref[...], preferred_element_type=F32)
    y = _ffn(y, gffn_ref, wup_ref, wdn_ref)
    if final_norm:
        y = _rms(y, gfin_ref[...])
    o_ref[...] = y


def _odd_layer(x, gmix, win, lng, lnb, ws, bs, wo, gffn, wup, wdn, gfin, emit_v, name):
    n = x.shape[0]
    tm = min(ROW_TILE, n)
    row = lambda i: (i, 0)
    final_norm = gfin is not None
    params = [gmix, win, lng, lnb, ws, bs, wo, gffn, wup, wdn]
    if final_norm:
        params.append(gfin)
    args = [x] + [_param_arg(a) for a in params]
    in_specs = [pl.BlockSpec((tm, D_MODEL), row)] + [_param_spec(a) for a in params]
    out_specs = [pl.BlockSpec((tm, D_MODEL), row)]
    out_shape = [jax.ShapeDtypeStruct((n, D_MODEL), F32)]
    if emit_v:
        out_specs.append(pl.BlockSpec((tm, D_MODEL), row))
        out_shape.append(jax.ShapeDtypeStruct((n, D_MODEL), F32))
    outs = pl.pallas_call(
        functools.partial(_odd_kernel, emit_v=emit_v, final_norm=final_norm),
        grid=(n // tm,),
        in_specs=in_specs,
        out_specs=out_specs,
        out_shape=out_shape,
        compiler_params=pltpu.CompilerParams(
            dimension_semantics=("arbitrary",), vmem_limit_bytes=VMEM_LIMIT),
        name=name,
    )(*args)
    return outs if emit_v else (outs[0], None)


def _rope_tables(pos):
    inv = ROPE_THETA ** (-jnp.arange(0, HEAD_DIM, 2, dtype=F32) / HEAD_DIM)
    ang = pos.astype(F32)[:, None] * inv[None, :]
    cos, sin = jnp.cos(ang), jnp.sin(ang)
    return (jnp.concatenate([cos, cos, cos, cos], axis=1),
            jnp.concatenate([-sin, sin, -sin, sin], axis=1))


def _sgu_weights(w_s, b_s, length):
    tril = jnp.tril(jnp.ones((length, length), dtype=bool))
    w = jnp.where(tril, w_s[:, :length, :length], 0.0)
    reps = SGU_CHUNK // length
    if reps > 1:
        eye = jnp.eye(reps, dtype=w.dtype)
        w = jnp.einsum('ab,gts->gatbs', eye, w).reshape(
            N_SGU_GROUPS, SGU_CHUNK, SGU_CHUNK)
    b = jnp.tile(b_s[:, :length], (1, reps))
    b = jnp.repeat(b.T, D_MODEL // N_SGU_GROUPS, axis=1)
    return w.astype(BF16), b


def kernel(x_prompt, x_sample, cache_k, cache_v, state_pool, page_table, norm_mix, w_in_even, w_o_even, w_pool, pool_scale, w_in_odd, sgu_ln_g, sgu_ln_b, sgu_w, sgu_b, w_o_odd, norm_ffn, w_ffn_up, w_ffn_down, norm_final):
    batch, seq, _ = x_prompt.shape
    n_seq, n_new, _ = x_sample.shape
    depth = norm_mix.shape[0]
    n_phys, page = cache_k.shape[1], cache_k.shape[2]
    past_len = page_table.shape[1] * page

    yp = x_prompt.reshape(batch * seq, D_MODEL)
    ys = x_sample.reshape(n_seq * n_new, D_MODEL)
    ck = jnp.transpose(cache_k, (0, 1, 3, 4, 2)).reshape(cache_k.shape[0], n_phys, ATTN_WIDTH, page)
    cv = jnp.transpose(cache_v, (0, 1, 3, 4, 2)).reshape(cache_v.shape[0], n_phys, ATTN_WIDTH, page)

    cos_p, sin_p = _rope_tables(jnp.arange(seq))
    cos_s, sin_s = _rope_tables(past_len + jnp.arange(n_seq * n_new) % n_new)

    w_pool_b = w_pool.astype(BF16)
    w_bf16 = {("in_even", 0): w_in_even[0].astype(BF16)}
    sample_k, sample_v = [], []
    prompt_pool, sample_pool, sample_sgu = [], [], []
    n_even = (depth + 1) // 2
    kv_t = None
    for l in range(depth):
        gmix = (norm_mix[:, None, :], l)
        gffn = (norm_ffn[:, None, :], l)
        if l % 2 == 0:
            i = l // 2
            pscale = (pool_scale[:, None, :], i)
            w_in_l, w_pool_l = w_bf16["in_even", i], (w_pool_b, i)
            q, k, kt_all, vt_all, p = _inproj_even(
                yp, gmix, w_in_l, cos_p, sin_p, f"inproj_p{l}",
                kv_t_shape=(n_even, batch, ATTN_WIDTH, seq), kv_t=kv_t, layer=i)
            kv_t = (kt_all, vt_all)
            jobs = {("up", l): (w_ffn_up, l), ("dn", l): (w_ffn_down, l),
                    ("o_even", i): (w_o_even, i)}
            if l + 1 < depth:
                jobs.update({("up", l + 1): (w_ffn_up, l + 1), ("dn", l + 1): (w_ffn_down, l + 1),
                             ("in_odd", i): (w_in_odd, i), ("o_odd", i): (w_o_odd, i)})
            if i + 1 < n_even:
                jobs["in_even", i + 1] = (w_in_even, i + 1)
            attn, *cast = _attn_prompt(q, k, vt_all, i, seq, f"attn_p{l}",
                                       cast=list(jobs.values()))
            w_bf16.update(zip(jobs.keys(), cast))
            w_o_l, w_up_l, w_dn_l = w_bf16["o_even", i], w_bf16["up", l], w_bf16["dn", l]
            yp = _mix_prompt(attn, p, yp, w_pool_l, pscale, w_o_l, gffn, w_up_l, w_dn_l,
                             seq, f"mix_p{l}")
            prompt_pool.append(p.reshape(batch, seq, POOL_WIDTH)[:, seq - POOL_HIST:, :])

            q, k, v, p = _inproj_even(ys, gmix, w_in_l, cos_s, sin_s, f"inproj_s{l}")
            attn = _attn_sample(q, k, v, n_new, ck, cv, page_table, i, f"attn_s{l}")
            hist = jnp.transpose(state_pool[i], (1, 0, 2))
            ys = _mix_sample(attn, p, hist, ys, w_pool_l,
                             pscale, w_o_l, gffn, w_up_l, w_dn_l, f"mix_s{l}")
            sample_k.append(k.reshape(n_seq, n_new, N_HEADS, HEAD_DIM))
            sample_v.append(v.reshape(n_seq, n_new, N_HEADS, HEAD_DIM))
            sample_pool.append(jnp.concatenate(
                [state_pool[i], p.reshape(n_seq, n_new, POOL_WIDTH)], axis=1)[:, -POOL_HIST:, :])
        else:
            j = l // 2
            gfin = norm_final[None, :] if l == depth - 1 else None
            lng = (sgu_ln_g[:, None, :], j)
            lnb = (sgu_ln_b[:, None, :], j)
            w_in_l, w_o_l = w_bf16["in_odd", j], w_bf16["o_odd", j]
            ws_p, bs_p = _sgu_weights(sgu_w[j], sgu_b[j], SGU_CHUNK)
            ws_s, bs_s = _sgu_weights(sgu_w[j], sgu_b[j], n_new)
            w_up_l, w_dn_l = w_bf16["up", l], w_bf16["dn", l]
            yp, _ = _odd_layer(yp, gmix, w_in_l, lng, lnb, ws_p, bs_p, w_o_l,
                               gffn, w_up_l, w_dn_l, gfin, False, f"odd_p{l}")
            ys, v_rows = _odd_layer(ys, gmix, w_in_l, lng, lnb, ws_s, bs_s, w_o_l,
                                    gffn, w_up_l, w_dn_l, gfin, True, f"odd_s{l}")
            sample_sgu.append(v_rows.reshape(n_seq, n_new, D_MODEL))
    if depth % 2 == 1:
        raise NotImplementedError("the final norm is fused into an odd last layer")
    prompt_k, prompt_v = [
        jnp.transpose(t.reshape(n_even, batch, N_HEADS, HEAD_DIM, seq), (0, 1, 4, 2, 3))
        for t in kv_t]
    return (yp.reshape(batch, seq, D_MODEL), ys.reshape(n_seq, n_new, D_MODEL),
            prompt_k, prompt_v,
            jnp.stack(sample_k), jnp.stack(sample_v),
            jnp.stack(prompt_pool), jnp.stack(sample_pool),
            jnp.stack(sample_sgu))
```
